```python
import math
import jax, jax.numpy as jnp
from jax import lax
import numpy as np

D_MODEL = 1024
BATCH = 8
SEQ = 2048
DEPTH = 4
DEC_BATCH = 32
DEC_SEQ = 1
PAST_LEN = 8192
PAGE_SIZE = 128

HEAD_DIM = 64
H_RET = 4
H_FOX = 8
C_GROUPS = 4
RET_W = H_RET * HEAD_DIM
FOX_W = H_FOX * HEAD_DIM
CMLP_W = C_GROUPS * HEAD_DIM
MIX_W = RET_W + FOX_W + CMLP_W
IN_COLS = 4 * RET_W + 3 * FOX_W + H_FOX + 2 * CMLP_W
CHUNK = 128
Q_BLOCK = 128
D_FF = 2816
CONV_W = 3
RMS_EPS = 1e-6
ROPE_BASE = 10000.0
FORGET_BIAS = 7.0
CACHE_LOGF_NOISE = 0.5

kernel_name = 'hybrid_retention_fox_chunkmlp_step'


def rms_norm(x, g):
    xf = x.astype(jnp.float32)
    y = xf * lax.rsqrt(jnp.mean(xf * xf, axis=-1, keepdims=True) + RMS_EPS)
    return (y * g.astype(jnp.float32)).astype(x.dtype)


def rotary(x, pos):
    half = HEAD_DIM // 2
    inv = 1.0 / (ROPE_BASE ** (jnp.arange(half, dtype=jnp.float32) / half))
    ang = pos.astype(jnp.float32)[:, None] * inv[None, :]
    cos = jnp.cos(ang)[None, :, None, :]
    sin = jnp.sin(ang)[None, :, None, :]
    xf = x.astype(jnp.float32)
    x1, x2 = xf[..., :half], xf[..., half:]
    return jnp.concatenate([x1 * cos - x2 * sin, x1 * sin + x2 * cos], axis=-1).astype(x.dtype)


def project(h, pos, w_in, b_f, q_g, k_g):
    B, L, _ = h.shape
    z = h @ w_in
    sizes = [RET_W] * 4 + [FOX_W] * 3 + [H_FOX] + [CMLP_W] * 2
    parts = []
    off = 0
    for s in sizes:
        parts.append(z[..., off:off + s])
        off += s
    rq, rk, rv, rg, fq, fk, fv, fz, cu, cv = parts
    heads = lambda a: a.reshape(B, L, -1, HEAD_DIM)
    rq = rotary(heads(rq), pos)
    rk = rotary(heads(rk), pos) * (HEAD_DIM ** -0.5)
    rv = heads(rv)
    fq = rms_norm(heads(fq), q_g)
    fk = rms_norm(heads(fk), k_g)
    fv = heads(fv)
    logf = jax.nn.log_sigmoid((fz + b_f).astype(jnp.float32))
    cu = jax.nn.gelu(cu)
    cv = jax.nn.gelu(cv)
    return rq, rk, rv, rg, fq, fk, fv, logf, cu, cv


def retention(q, k, v, s0):
    B, L = q.shape[:2]
    c = math.gcd(L, CHUNK)
    n = L // c
    lg = jnp.log1p(-jnp.exp2(-5.0 - jnp.arange(H_RET, dtype=jnp.float32)))
    idx = jnp.arange(c, dtype=jnp.float32)
    diff = idx[:, None] - idx[None, :]
    intra = jnp.where(diff[None] >= 0, jnp.exp(jnp.maximum(diff, 0.0)[None] * lg[:, None, None]), 0.0)
    q_dec = jnp.exp((idx[None, :] + 1.0) * lg[:, None])
    k_dec = jnp.exp((c - 1.0 - idx[None, :]) * lg[:, None])
    chunk_dec = jnp.exp(c * lg)

    def to_chunks(a):
        return a.reshape(B, n, c, H_RET, HEAD_DIM).transpose(1, 0, 3, 2, 4).astype(jnp.float32)

    def step(s, inp):
        qc, kc, vc = inp
        sc = jnp.einsum('bhtd,bhsd->bhts', qc, kc) * intra
        o = (jnp.einsum('bhts,bhsd->bhtd', sc, vc)
             + jnp.einsum('bhtd,bhde->bhte', qc * q_dec[None, :, :, None], s))
        s_new = (s * chunk_dec[None, :, None, None]
                 + jnp.einsum('bhsd,bhse->bhde', kc * k_dec[None, :, :, None], vc))
        return s_new, o

    s_fin, o = lax.scan(step, s0.astype(jnp.float32), (to_chunks(q), to_chunks(k), to_chunks(v)))
    o = o.transpose(1, 0, 3, 2, 4).reshape(B, L, H_RET, HEAD_DIM)
    return o.astype(v.dtype), s_fin


def fox_prompt(q, k, v, logf):
    B, L = q.shape[:2]
    scale = HEAD_DIM ** -0.5
    cum = jnp.cumsum(logf, axis=1).transpose(0, 2, 1)
    kpos = jnp.arange(L)

    def block(i):
        qb = lax.dynamic_slice_in_dim(q, i * Q_BLOCK, Q_BLOCK, axis=1)
        cb = lax.dynamic_slice_in_dim(cum, i * Q_BLOCK, Q_BLOCK, axis=2)
        qpos = i * Q_BLOCK + jnp.arange(Q_BLOCK)
        s = jnp.einsum('bthd,bshd->bhts', qb, k, preferred_element_type=jnp.float32) * scale
        s = s + cb[..., :, None] - cum[..., None, :]
        s = jnp.where(qpos[:, None] >= kpos[None, :], s, -jnp.inf)
        p = jax.nn.softmax(s, axis=-1)
        return jnp.einsum('bhts,bshd->bthd', p.astype(v.dtype), v)

    o = lax.map(block, jnp.arange(L // Q_BLOCK))
    return o.transpose(1, 0, 2, 3, 4).reshape(B, L, H_FOX, HEAD_DIM)


def fox_sample(q, k, v, logf, k_past, v_past, logf_past):
    T = q.shape[1]
    P = k_past.shape[1]
    scale = HEAD_DIM ** -0.5
    c_new = jnp.cumsum(logf, axis=1).transpose(0, 2, 1)
    lp = logf_past.astype(jnp.float32)
    suffix = (lax.cumsum(lp, axis=1, reverse=True) - lp).transpose(0, 2, 1)
    s_past = jnp.einsum('bthd,bshd->bhts', q, k_past, preferred_element_type=jnp.float32) * scale
    s_past = s_past + c_new[..., :, None] + suffix[..., None, :]
    s_new = jnp.einsum('bthd,bshd->bhts', q, k, preferred_element_type=jnp.float32) * scale
    s_new = s_new + c_new[..., :, None] - c_new[..., None, :]
    causal = jnp.tril(jnp.ones((T, T), dtype=bool))
    s_new = jnp.where(causal, s_new, -jnp.inf)
    p = jax.nn.softmax(jnp.concatenate([s_past, s_new], axis=-1), axis=-1)
    o = (jnp.einsum('bhts,bshd->bthd', p[..., :P].astype(v.dtype), v_past)
         + jnp.einsum('bhts,bshd->bthd', p[..., P:].astype(v.dtype), v))
    return o


def chunk_mlp(u, v, w_s, b_s):
    B, L = v.shape[:2]
    Lp = -(-L // CHUNK) * CHUNK
    vp = jnp.pad(v, ((0, 0), (0, Lp - L), (0, 0))).reshape(B, Lp // CHUNK, CHUNK, C_GROUPS, HEAD_DIM)
    mask = jnp.tril(jnp.ones((CHUNK, CHUNK), dtype=bool))
    w = jnp.where(mask[None], w_s, 0.0).astype(v.dtype)
    mixed = jnp.einsum('gts,bnsgc->bntgc', w, vp) + b_s.T[None, None, :, :, None]
    mixed = mixed.reshape(B, Lp, CMLP_W)[:, :L]
    return u * mixed


def merge(ret_o, rg, ret_g, fox_o, cu, cv, sp_w, sp_b, w_out):
    B, L = rg.shape[:2]
    y_ret = jax.nn.silu(rg) * rms_norm(ret_o, ret_g).reshape(B, L, RET_W)
    y_fox = fox_o.reshape(B, L, FOX_W)
    y_c = chunk_mlp(cu, cv, sp_w, sp_b)
    return jnp.concatenate([y_ret, y_fox, y_c], axis=-1) @ w_out


def conv_ffn(h, prev, w_up, conv_w, conv_b, w_down):
    L = h.shape[1]
    up = h @ w_up
    hp = jnp.concatenate([prev.astype(up.dtype), up], axis=1)
    conv = conv_b
    for j in range(CONV_W):
        conv = conv + conv_w[j] * hp[:, j:j + L]
    gate, val = jnp.split(conv, 2, axis=-1)
    return (jax.nn.silu(gate) * val) @ w_down, hp[:, -(CONV_W - 1):]


def setup_inputs(seed: int = 0) -> dict:
    key = jax.random.key(seed)
    ks = jax.random.split(key, 32)
    n_pages = PAST_LEN // PAGE_SIZE
    n_used = DEC_BATCH * n_pages
    n_phys = n_used + n_used // 4
    perm = jax.random.permutation(ks[0], n_phys)
    page_table = perm[:n_used].reshape(DEC_BATCH, n_pages).astype(jnp.int32)

    def nrm(k, shape, scale):
        return jax.random.normal(k, shape, jnp.float32) * scale

    return {
        'x_prompt': nrm(ks[1], (BATCH, SEQ, D_MODEL), 1.0),
        'x_sample': nrm(ks[2], (DEC_BATCH, DEC_SEQ, D_MODEL), 1.0),
        'cache_k': nrm(ks[3], (DEPTH, n_phys, PAGE_SIZE, H_FOX, HEAD_DIM), 1.0),
        'cache_v': nrm(ks[4], (DEPTH, n_phys, PAGE_SIZE, H_FOX, HEAD_DIM), 1.0),
        'cache_logf': jax.nn.log_sigmoid(FORGET_BIAS + nrm(ks[5], (DEPTH, n_phys, PAGE_SIZE, H_FOX), CACHE_LOGF_NOISE)),
        'state_ret': nrm(ks[6], (DEPTH, DEC_BATCH, H_RET, HEAD_DIM, HEAD_DIM), 0.5),
        'state_ffn_conv': nrm(ks[7], (DEPTH, DEC_BATCH, CONV_W - 1, 2 * D_FF), 1.0),
        'page_table': page_table,
        'norm1_g': 1.0 + nrm(ks[8], (DEPTH, D_MODEL), 0.1),
        'w_in': nrm(ks[9], (DEPTH, D_MODEL, IN_COLS), D_MODEL ** -0.5),
        'b_forget': FORGET_BIAS + nrm(ks[10], (DEPTH, H_FOX), 0.1),
        'ret_norm_g': 1.0 + nrm(ks[11], (DEPTH, H_RET, HEAD_DIM), 0.1),
        'q_norm_g': 1.0 + nrm(ks[12], (DEPTH, HEAD_DIM), 0.1),
        'k_norm_g': 1.0 + nrm(ks[13], (DEPTH, HEAD_DIM), 0.1),
        'spatial_w': nrm(ks[14], (DEPTH, C_GROUPS, CHUNK, CHUNK), 0.5 * CHUNK ** -0.5),
        'spatial_b': 1.0 + nrm(ks[15], (DEPTH, C_GROUPS, CHUNK), 0.1),
        'w_out': nrm(ks[16], (DEPTH, MIX_W, D_MODEL), MIX_W ** -0.5),
        'norm2_g': 1.0 + nrm(ks[17], (DEPTH, D_MODEL), 0.1),
        'w_up': nrm(ks[18], (DEPTH, D_MODEL, 2 * D_FF), D_MODEL ** -0.5),
        'conv_w': nrm(ks[19], (DEPTH, CONV_W, 2 * D_FF), CONV_W ** -0.5),
        'conv_b': nrm(ks[20], (DEPTH, 2 * D_FF), 0.01),
        'w_down': nrm(ks[21], (DEPTH, D_FF, D_MODEL), D_FF ** -0.5),
    }


def reference(x_prompt, x_sample, cache_k, cache_v, cache_logf, state_ret, state_ffn_conv, page_table,
              norm1_g, w_in, b_forget, ret_norm_g, q_norm_g, k_norm_g, spatial_w, spatial_b, w_out,
              norm2_g, w_up, conv_w, conv_b, w_down):
    bp, lp, _ = x_prompt.shape
    bs, ls, _ = x_sample.shape
    pos_p = jnp.arange(lp)
    pos_s = PAST_LEN + jnp.arange(ls)
    xp, xs = x_prompt, x_sample
    kp_l, vp_l, fp_l, rp_l, cp_l = [], [], [], [], []
    ks_l, vs_l, fs_l, rs_l, chs_l, cs_l = [], [], [], [], [], []
    for l in range(DEPTH):
        h = rms_norm(xp, norm1_g[l])
        rq, rk, rv, rg, fq, fk, fv, lf, cu, cv = project(h, pos_p, w_in[l], b_forget[l], q_norm_g[l], k_norm_g[l])
        ro, r_state = retention(rq, rk, rv, jnp.zeros((bp, H_RET, HEAD_DIM, HEAD_DIM), jnp.float32))
        fo = fox_prompt(fq, fk, fv, lf)
        xp = xp + merge(ro, rg, ret_norm_g[l], fo, cu, cv, spatial_w[l], spatial_b[l], w_out[l])
        f, c_state = conv_ffn(rms_norm(xp, norm2_g[l]), jnp.zeros((bp, CONV_W - 1, 2 * D_FF), xp.dtype),
                              w_up[l], conv_w[l], conv_b[l], w_down[l])
        xp = xp + f
        kp_l.append(fk)
        vp_l.append(fv)
        fp_l.append(lf.astype(cache_logf.dtype))
        rp_l.append(r_state.astype(state_ret.dtype))
        cp_l.append(c_state)

        h = rms_norm(xs, norm1_g[l])
        rq, rk, rv, rg, fq, fk, fv, lf, cu, cv = project(h, pos_s, w_in[l], b_forget[l], q_norm_g[l], k_norm_g[l])
        ro, r_state = retention(rq, rk, rv, state_ret[l])
        k_past = cache_k[l][page_table].reshape(bs, -1, H_FOX, HEAD_DIM)
        v_past = cache_v[l][page_table].reshape(bs, -1, H_FOX, HEAD_DIM)
        f_past = cache_logf[l][page_table].reshape(bs, -1, H_FOX)
        fo = fox_sample(fq, fk, fv, lf, k_past, v_past, f_past)
        xs = xs + merge(ro, rg, ret_norm_g[l], fo, cu, cv, spatial_w[l], spatial_b[l], w_out[l])
        f, c_state = conv_ffn(rms_norm(xs, norm2_g[l]), state_ffn_conv[l], w_up[l], conv_w[l], conv_b[l], w_down[l])
        xs = xs + f
        ks_l.append(fk)
        vs_l.append(fv)
        fs_l.append(lf.astype(cache_logf.dtype))
        rs_l.append(r_state.astype(state_ret.dtype))
        chs_l.append(cv)
        cs_l.append(c_state)

    return (xp, xs,
            jnp.stack(kp_l), jnp.stack(vp_l), jnp.stack(fp_l), jnp.stack(rp_l), jnp.stack(cp_l),
            jnp.stack(ks_l), jnp.stack(vs_l), jnp.stack(fs_l), jnp.stack(rs_l), jnp.stack(chs_l), jnp.stack(cs_l))
```

```python
import functools
import math

import jax
import jax.numpy as jnp
from jax import lax
from jax.experimental import pallas as pl
from jax.experimental.pallas import tpu as pltpu

HEAD_DIM = 64
H_RET = 4
H_FOX = 8
C_GROUPS = 4
RET_W = H_RET * HEAD_DIM
FOX_W = H_FOX * HEAD_DIM
CMLP_W = C_GROUPS * HEAD_DIM
CHUNK = 128
PAGE_SIZE = 128
CONV_W = 3
RMS_EPS = 1e-6
ROPE_BASE = 10000.0
ATTN_SCALE = HEAD_DIM ** -0.5

LANES = 128
FZ_COLS = LANES
VMEM_LIMIT = 56 * 1024 * 1024
NEG_BIG = -1e30

F32 = jnp.float32
BF16 = jnp.bfloat16


def _dot(a, b):
    return jnp.dot(a, b, preferred_element_type=F32)


def _dot_nt(a, b):
    return lax.dot_general(a, b, (((1,), (1,)), ((), ())), preferred_element_type=F32)


def _split2(x):
    hi = x.astype(BF16)
    lo = (x - hi.astype(F32)).astype(BF16)
    return hi, lo


def _split3(x):
    a = x.astype(BF16)
    r = x - a.astype(F32)
    b = r.astype(BF16)
    c = (r - b.astype(F32)).astype(BF16)
    return a, b, c


def _seg_mean(sq, seg):
    hi, lo = _split2(sq)
    return (_dot(hi, seg) + _dot(lo, seg)) * (1.0 / HEAD_DIM)


def _head_rms(x, seg, g):
    parts = []
    for c in range(x.shape[-1] // 256):
        xc = x[:, c * 256:(c + 1) * 256]
        parts.append(xc * lax.rsqrt(_seg_mean(xc * xc, seg) + RMS_EPS))
    y = parts[0] if len(parts) == 1 else jnp.concatenate(parts, axis=-1)
    return y * g


def _row_rms(x, g):
    return x * lax.rsqrt(jnp.mean(x * x, axis=-1, keepdims=True) + RMS_EPS) * g


def _proj_kernel(x_ref, cos_ref, sin_ref, g1_ref, w_ref, bf_ref, qg_ref, kg_ref, seg_ref,
                 rq_ref, rk_ref, rv_ref, rg_ref, fq_ref, fk_ref, fv_ref, lf_ref, lfp_ref,
                 cu_ref, cv_ref):
    hb = _row_rms(x_ref[...], g1_ref[...]).astype(BF16)
    seg = seg_ref[...]

    def proj(lo, n):
        return _dot(hb, w_ref[:, lo:lo + n])

    cos = cos_ref[...]
    sin = sin_ref[...]
    lane = lax.broadcasted_iota(jnp.int32, cos.shape, 1)
    first_half = (lane % HEAD_DIM) < (HEAD_DIM // 2)

    def rot(z):
        swapped = jnp.where(first_half, pltpu.roll(z, RET_W - HEAD_DIM // 2, 1),
                            pltpu.roll(z, HEAD_DIM // 2, 1))
        return z * cos + swapped * sin

    rq_ref[...] = rot(proj(0, RET_W))
    rk_ref[...] = rot(proj(RET_W, RET_W)) * ATTN_SCALE
    rv_ref[...] = proj(2 * RET_W, RET_W)
    rg_ref[...] = proj(3 * RET_W, RET_W)
    off = 4 * RET_W
    fq_ref[...] = _head_rms(proj(off, FOX_W), seg, qg_ref[...])
    fk_ref[...] = _head_rms(proj(off + FOX_W, FOX_W), seg, kg_ref[...])
    fv_ref[...] = proj(off + 2 * FOX_W, FOX_W)
    off += 3 * FOX_W
    cu_ref[...] = jax.nn.gelu(proj(off, CMLP_W))
    cv_ref[...] = jax.nn.gelu(proj(off + CMLP_W, CMLP_W))
    off += 2 * CMLP_W
    lf = jax.nn.log_sigmoid(proj(off, FZ_COLS) + bf_ref[...])
    lane_z = lax.broadcasted_iota(jnp.int32, lf.shape, 1)
    lfp_ref[...] = jnp.where(lane_z < H_FOX, lf, 0.0)
    lf_ref[...] = lf[:, :H_FOX]


def _proj(x2d, cos, sin, g1, w, bfp, qg, kg, seg, tm):
    n, d = x2d.shape
    nt = n // tm
    tt = cos.shape[0] // tm
    row = lambda i: (i, 0)
    const = lambda i: (0, 0)
    outs = [RET_W] * 4 + [FOX_W] * 3 + [H_FOX, FZ_COLS] + [CMLP_W] * 2
    return pl.pallas_call(
        _proj_kernel,
        grid=(nt,),
        in_specs=[
            pl.BlockSpec((tm, d), row),
            pl.BlockSpec((tm, RET_W), lambda i: (i % tt, 0)),
            pl.BlockSpec((tm, RET_W), lambda i: (i % tt, 0)),
            pl.BlockSpec((1, d), const),
            pl.BlockSpec(w.shape, const),
            pl.BlockSpec((1, FZ_COLS), const),
            pl.BlockSpec((1, FOX_W), const),
            pl.BlockSpec((1, FOX_W), const),
            pl.BlockSpec((256, 256), const),
        ],
        out_specs=[pl.BlockSpec((tm, c), row) for c in outs],
        out_shape=[jax.ShapeDtypeStruct((n, c), F32) for c in outs],
        compiler_params=pltpu.CompilerParams(
            dimension_semantics=("arbitrary",), vmem_limit_bytes=VMEM_LIMIT),
        name="proj",
    )(x2d, cos, sin, g1, w, bfp, qg, kg, seg)


def _ret_prompt_kernel(q_ref, k_ref, v_ref, intra_ref, qdec_ref, kdec_ref, dec_ref, bd_ref,
                       o_ref, st_ref, s_sc):
    c = pl.program_id(1)

    @pl.when(c == 0)
    def _():
        s_sc[...] = jnp.zeros_like(s_sc)

    q = q_ref[...]
    k = k_ref[...]
    vb = v_ref[...].astype(BF16)
    kb = k.astype(BF16)
    lane = lax.broadcasted_iota(jnp.int32, q.shape, 1)
    s_old = s_sc[...]
    o = _dot((q * qdec_ref[...]).astype(BF16), s_old.astype(BF16))
    for h in range(H_RET):
        hm = (lane // HEAD_DIM) == h
        sc = _dot_nt(jnp.where(hm, q, 0.0).astype(BF16), kb) * intra_ref[h]
        o = o + jnp.where(hm, _dot(sc.astype(BF16), vb), 0.0)
    o_ref[...] = o
    kd = (k * kdec_ref[...]).astype(BF16)
    ktv = lax.dot_general(kd, vb, (((0,), (0,)), ((), ())), preferred_element_type=F32)
    s_new = s_old * dec_ref[...] + bd_ref[...] * ktv
    s_sc[...] = s_new

    @pl.when(c == pl.num_programs(1) - 1)
    def _():
        for h in range(H_RET):
            lo = h * HEAD_DIM
            st_ref[h] = s_new[lo:lo + HEAD_DIM, lo:lo + HEAD_DIM]


def _ret_prompt(rq, rk, rv, tabs, bsz, seq):
    nc = seq // CHUNK
    blk = pl.BlockSpec((CHUNK, RET_W), lambda b, c: (b * nc + c, 0))
    c2 = lambda b, c: (0, 0)
    return pl.pallas_call(
        _ret_prompt_kernel,
        grid=(bsz, nc),
        in_specs=[blk, blk, blk,
                  pl.BlockSpec((H_RET, CHUNK, CHUNK), lambda b, c: (0, 0, 0)),
                  pl.BlockSpec((CHUNK, RET_W), c2),
                  pl.BlockSpec((CHUNK, RET_W), c2),
                  pl.BlockSpec((RET_W, RET_W), c2),
                  pl.BlockSpec((RET_W, RET_W), c2)],
        out_specs=[blk, pl.BlockSpec((None, H_RET, HEAD_DIM, HEAD_DIM), lambda b, c: (b, 0, 0, 0))],
        out_shape=[jax.ShapeDtypeStruct((bsz * seq, RET_W), F32),
                   jax.ShapeDtypeStruct((bsz, H_RET, HEAD_DIM, HEAD_DIM), F32)],
        scratch_shapes=[pltpu.VMEM((RET_W, RET_W), F32)],
        compiler_params=pltpu.CompilerParams(
            dimension_semantics=("arbitrary", "arbitrary"), vmem_limit_bytes=VMEM_LIMIT),
        name="ret_prompt",
    )(rq, rk, rv, tabs["intra"], tabs["qdec"], tabs["kdec"], tabs["decmat"], tabs["bd"])


def _ret_sample_kernel(q_ref, k_ref, v_ref, s_ref, gcol_ref, grow_ref, o_ref, st_ref):
    for h in range(H_RET):
        qc = q_ref[h]
        kc = k_ref[h]
        vr = v_ref[h]
        s = s_ref[h]
        qk = jnp.sum(qc * kc, axis=0, keepdims=True)
        inter = jnp.sum((qc * gcol_ref[h]) * s, axis=0, keepdims=True)
        o_ref[h] = qk * vr + inter
        st_ref[h] = s * grow_ref[h] + kc * vr


def _ret_sample(rq, rk, rv, state, layer, tabs):
    bsz = rq.shape[0]
    qc = rq.reshape(bsz, H_RET, HEAD_DIM, 1)
    kc = rk.reshape(bsz, H_RET, HEAD_DIM, 1)
    vr = rv.reshape(bsz, H_RET, 1, HEAD_DIM)
    col = pl.BlockSpec((None, H_RET, HEAD_DIM, 1), lambda b: (b, 0, 0, 0))
    rowb = pl.BlockSpec((None, H_RET, 1, HEAD_DIM), lambda b: (b, 0, 0, 0))
    stb = pl.BlockSpec((None, H_RET, HEAD_DIM, HEAD_DIM), lambda b: (b, 0, 0, 0))
    o, st = pl.pallas_call(
        _ret_sample_kernel,
        grid=(bsz,),
        in_specs=[col, col, rowb,
                  pl.BlockSpec((None, None, H_RET, HEAD_DIM, HEAD_DIM), lambda b: (layer, b, 0, 0, 0)),
                  pl.BlockSpec((H_RET, HEAD_DIM, 1), lambda b: (0, 0, 0)),
                  pl.BlockSpec((H_RET, 1, HEAD_DIM), lambda b: (0, 0, 0))],
        out_specs=[rowb, stb],
        out_shape=[jax.ShapeDtypeStruct((bsz, H_RET, 1, HEAD_DIM), F32),
                   jax.ShapeDtypeStruct((bsz, H_RET, HEAD_DIM, HEAD_DIM), F32)],
        compiler_params=pltpu.CompilerParams(dimension_semantics=("arbitrary",)),
        name="ret_sample",
    )(qc, kc, vr, state, tabs["gcol"], tabs["grow"])
    return o.reshape(bsz, RET_W), st


def _cum_kernel(lf_ref, tri_ref, cum_ref, cumt_ref):
    tri = tri_ref[...]
    carry = jnp.zeros((1, LANES), F32)
    for c in range(lf_ref.shape[0] // CHUNK):
        a, b, r = _split3(lf_ref[c * CHUNK:(c + 1) * CHUNK, :])
        cc = (_dot(tri, a) + _dot(tri, b)) + _dot(tri, r) + carry
        cum_ref[c * CHUNK:(c + 1) * CHUNK, :] = cc
        cumt_ref[:, c * CHUNK:(c + 1) * CHUNK] = cc.T[:H_FOX, :]
        carry = cc[CHUNK - 1:CHUNK, :]


def _cum(lfp, tri, bsz, seq):
    return pl.pallas_call(
        _cum_kernel,
        grid=(bsz,),
        in_specs=[pl.BlockSpec((seq, LANES), lambda b: (b, 0)),
                  pl.BlockSpec((CHUNK, CHUNK), lambda b: (0, 0))],
        out_specs=[pl.BlockSpec((seq, LANES), lambda b: (b, 0)),
                   pl.BlockSpec((None, H_FOX, seq), lambda b: (b, 0, 0))],
        out_shape=[jax.ShapeDtypeStruct((bsz * seq, LANES), F32),
                   jax.ShapeDtypeStruct((bsz, H_FOX, seq), F32)],
        compiler_params=pltpu.CompilerParams(dimension_semantics=("arbitrary",)),
        name="cum",
    )(lfp, tri)


def _fox_prompt_kernel(q_ref, k_ref, v_ref, cum_ref, cumt_ref, o_ref, *, tq, tk):
    hp = pl.program_id(1)
    i = pl.program_id(2)
    q = q_ref[...] * ATTN_SCALE
    cum = cum_ref[...]
    lane = lax.broadcasted_iota(jnp.int32, q.shape, 1)
    rows = i * tq + lax.broadcasted_iota(jnp.int32, (tq, tk), 0)
    cols0 = lax.broadcasted_iota(jnp.int32, (tq, tk), 1)
    nk = (i * tq + tq + tk - 1) // tk
    outs = []
    for hh in range(2):
        head = 2 * hp + hh
        hm = (lane // HEAD_DIM) == hh
        qh = jnp.where(hm, q, 0.0).astype(BF16)
        cq = jnp.sum(jnp.where(lane == head, cum, 0.0), axis=-1, keepdims=True)

        def body(j, carry, qh=qh, cq=cq, head=head):
            m, l, acc = carry
            k0 = pl.multiple_of(j * tk, tk)
            kb = k_ref[pl.ds(k0, tk), :].astype(BF16)
            vb = v_ref[pl.ds(k0, tk), :].astype(BF16)
            ck = cumt_ref[pl.ds(head, 1), pl.ds(k0, tk)]
            s = _dot_nt(qh, kb) + cq - ck
            s = jnp.where(rows >= cols0 + j * tk, s, -jnp.inf)
            m_new = jnp.maximum(m, jnp.max(s, axis=-1, keepdims=True))
            alpha = jnp.exp(m - m_new)
            p = jnp.exp(s - m_new)
            l = alpha * l + jnp.sum(p, axis=-1, keepdims=True)
            acc = alpha * acc + _dot(p.astype(BF16), vb)
            return m_new, l, acc

        m0 = jnp.full((tq, 1), NEG_BIG, F32)
        l0 = jnp.zeros((tq, 1), F32)
        a0 = jnp.zeros((tq, LANES), F32)
        _, l, acc = lax.fori_loop(0, nk, body, (m0, l0, a0))
        outs.append(acc / l)
    o_ref[...] = jnp.where((lane // HEAD_DIM) == 0, outs[0], outs[1])


def _fox_prompt(fq, fk, fv, cum, cumt, bsz, seq, tq, tk):
    nq = seq // tq
    npair = H_FOX // 2
    return pl.pallas_call(
        functools.partial(_fox_prompt_kernel, tq=tq, tk=tk),
        grid=(bsz, npair, nq),
        in_specs=[pl.BlockSpec((tq, LANES), lambda b, h, i: (b * nq + i, h)),
                  pl.BlockSpec((seq, LANES), lambda b, h, i: (b, h)),
                  pl.BlockSpec((seq, LANES), lambda b, h, i: (b, h)),
                  pl.BlockSpec((tq, LANES), lambda b, h, i: (b * nq + i, 0)),
                  pl.BlockSpec((None, H_FOX, seq), lambda b, h, i: (b, 0, 0))],
        out_specs=pl.BlockSpec((tq, LANES), lambda b, h, i: (b * nq + i, h)),
        out_shape=jax.ShapeDtypeStruct((bsz * seq, FOX_W), F32),
        compiler_params=pltpu.CompilerParams(
            dimension_semantics=("arbitrary", "arbitrary", "arbitrary"),
            vmem_limit_bytes=VMEM_LIMIT),
        name="fox_prompt",
    )(fq, fk, fv, cum, cumt)


def _fox_sample_kernel(pt_ref, q_ref, kn_ref, vn_ref, lfn_ref, w_ref, *rest, pages_per_step):
    del pt_ref
    g = pages_per_step
    k_refs = rest[:g]
    v_refs = rest[g:2 * g]
    lf_refs = rest[2 * g:3 * g]
    o_ref = rest[3 * g]
    m_sc, l_sc, acc_sc, after_sc = rest[3 * g + 1:]
    jg = pl.program_id(1)

    sub = lax.broadcasted_iota(jnp.int32, (H_FOX, FOX_W), 0)
    lane = lax.broadcasted_iota(jnp.int32, (H_FOX, FOX_W), 1)
    hmask = (lane // HEAD_DIM) == sub
    qrows = jnp.where(hmask, q_ref[...] * ATTN_SCALE, 0.0)

    @pl.when(jg == 0)
    def _():
        m_sc[...] = jnp.full_like(m_sc, NEG_BIG)
        l_sc[...] = jnp.zeros_like(l_sc)
        acc_sc[...] = jnp.zeros_like(acc_sc)
        after_sc[...] = jnp.zeros_like(after_sc)

    qb = qrows.astype(BF16)
    w = w_ref[...]
    m = m_sc[...]
    l = l_sc[...]
    acc = acc_sc[...]
    after = after_sc[...]
    for t in range(g):
        kb = k_refs[t][...].astype(BF16)
        vb = v_refs[t][...].astype(BF16)
        a, b, c = _split3(lf_refs[t][...])
        st = (_dot(a, w) + _dot(b, w)) + _dot(c, w)
        s = _dot(qb, kb) + st[:, :PAGE_SIZE] + after
        after = after + st[:, PAGE_SIZE:]
        m_new = jnp.maximum(m, jnp.max(s, axis=-1, keepdims=True))
        alpha = jnp.exp(m - m_new)
        p = jnp.exp(s - m_new)
        l = alpha * l + jnp.sum(p, axis=-1, keepdims=True)
        acc = alpha * acc + _dot_nt(p.astype(BF16), vb)
        m = m_new
    m_sc[...] = m
    l_sc[...] = l
    acc_sc[...] = acc
    after_sc[...] = after

    @pl.when(jg == pl.num_programs(1) - 1)
    def _():
        s_new = jnp.sum(qrows * kn_ref[...], axis=-1, keepdims=True) - lfn_ref[...]
        m_fin = jnp.maximum(m, s_new)
        alpha = jnp.exp(m - m_fin)
        p_new = jnp.exp(s_new - m_fin)
        l_fin = alpha * l + p_new
        acc_fin = alpha * acc + p_new * vn_ref[...]
        out = jnp.where(hmask, acc_fin / l_fin, 0.0)
        o_ref[...] = jnp.sum(out, axis=0, keepdims=True)


def _fox_sample(fq, fk, fv, lf, ck_t, cv_t, clf_t, page_table, layer, w_sfx, g):
    bsz = fq.shape[0]
    n_pages = page_table.shape[1]
    n_steps = n_pages // g
    q3 = fq.reshape(bsz, 1, FOX_W)
    k3 = fk.reshape(bsz, 1, FOX_W)
    v3 = fv.reshape(bsz, 1, FOX_W)
    lf3 = lf.reshape(bsz, H_FOX, 1)
    row = pl.BlockSpec((None, 1, FOX_W), lambda b, j, pt: (b, 0, 0))

    def page_spec(rows, t):
        def imap(b, j, pt):
            return (layer, pt[b, n_pages - 1 - (j * g + t)], 0, 0)
        return pl.BlockSpec((None, None, rows, PAGE_SIZE), imap)

    in_specs = [row, row, row,
                pl.BlockSpec((None, H_FOX, 1), lambda b, j, pt: (b, 0, 0)),
                pl.BlockSpec(w_sfx.shape, lambda b, j, pt: (0, 0))]
    in_specs += [page_spec(FOX_W, t) for t in range(g)]
    in_specs += [page_spec(FOX_W, t) for t in range(g)]
    in_specs += [page_spec(H_FOX, t) for t in range(g)]
    out = pl.pallas_call(
        functools.partial(_fox_sample_kernel, pages_per_step=g),
        grid_spec=pltpu.PrefetchScalarGridSpec(
            num_scalar_prefetch=1,
            grid=(bsz, n_steps),
            in_specs=in_specs,
            out_specs=row,
            scratch_shapes=[pltpu.VMEM((H_FOX, 1), F32), pltpu.VMEM((H_FOX, 1), F32),
                            pltpu.VMEM((H_FOX, FOX_W), F32), pltpu.VMEM((H_FOX, PAGE_SIZE), F32)]),
        out_shape=jax.ShapeDtypeStruct((bsz, 1, FOX_W), F32),
        compiler_params=pltpu.CompilerParams(
            dimension_semantics=("arbitrary", "arbitrary"), vmem_limit_bytes=VMEM_LIMIT),
        name="fox_sample",
    )(page_table, q3, k3, v3, lf3, w_sfx, *([ck_t] * g), *([cv_t] * g), *([clf_t] * g))
    return out.reshape(bsz, FOX_W)


def _merge_out(x_ref, ro_ref, rg_ref, fo_ref, y_c, seg_ref, retg_ref, wout_ref, out_ref):
    y_ret = jax.nn.silu(rg_ref[...]) * _head_rms(ro_ref[...], seg_ref[...], retg_ref[...])
    y = jnp.concatenate([y_ret, fo_ref[...], y_c], axis=-1).astype(BF16)
    out_ref[...] = x_ref[...] + _dot(y, wout_ref[...])


def _merge_prompt_kernel(x_ref, ro_ref, rg_ref, fo_ref, cu_ref, cv_ref, seg_ref, retg_ref,
                         sw_ref, sb_ref, wout_ref, out_ref):
    tm = x_ref.shape[0]
    r = lax.broadcasted_iota(jnp.int32, (CHUNK, CHUNK), 0)
    c = lax.broadcasted_iota(jnp.int32, (CHUNK, CHUNK), 1)
    lane = lax.broadcasted_iota(jnp.int32, (CHUNK, CMLP_W), 1)
    ws = [jnp.where(r >= c, sw_ref[g], 0.0).astype(BF16) for g in range(C_GROUPS)]
    sb = sb_ref[...]
    chunks = []
    for n in range(tm // CHUNK):
        vb = cv_ref[n * CHUNK:(n + 1) * CHUNK, :].astype(BF16)
        mixed = sb
        for g in range(C_GROUPS):
            mixed = mixed + jnp.where((lane // HEAD_DIM) == g, _dot(ws[g], vb), 0.0)
        chunks.append(cu_ref[n * CHUNK:(n + 1) * CHUNK, :] * mixed)
    y_c = chunks[0] if len(chunks) == 1 else jnp.concatenate(chunks, axis=0)
    _merge_out(x_ref, ro_ref, rg_ref, fo_ref, y_c, seg_ref, retg_ref, wout_ref, out_ref)


def _merge_sample_kernel(x_ref, ro_ref, rg_ref, fo_ref, cu_ref, cv_ref, seg_ref, retg_ref,
                         sw0_ref, sb0_ref, wout_ref, out_ref):
    y_c = cu_ref[...] * (sw0_ref[...] * cv_ref[...] + sb0_ref[...])
    _merge_out(x_ref, ro_ref, rg_ref, fo_ref, y_c, seg_ref, retg_ref, wout_ref, out_ref)


def _merge(x2d, ro, rg, fo, cu, cv, seg, retg, sw, sb, wout, tm, sample):
    n, d = x2d.shape
    row = lambda i: (i, 0)
    const = lambda i: (0, 0)
    if sample:
        kern = _merge_sample_kernel
        sw_spec = pl.BlockSpec((1, CMLP_W), const)
        sb_spec = pl.BlockSpec((1, CMLP_W), const)
    else:
        kern = _merge_prompt_kernel
        sw_spec = pl.BlockSpec((C_GROUPS, CHUNK, CHUNK), lambda i: (0, 0, 0))
        sb_spec = pl.BlockSpec((CHUNK, CMLP_W), const)
    return pl.pallas_call(
        kern,
        grid=(n // tm,),
        in_specs=[pl.BlockSpec((tm, d), row),
                  pl.BlockSpec((tm, RET_W), row), pl.BlockSpec((tm, RET_W), row),
                  pl.BlockSpec((tm, FOX_W), row),
                  pl.BlockSpec((tm, CMLP_W), row), pl.BlockSpec((tm, CMLP_W), row),
                  pl.BlockSpec((256, 256), const), pl.BlockSpec((1, RET_W), const),
                  sw_spec, sb_spec,
                  pl.BlockSpec(wout.shape, const)],
        out_specs=pl.BlockSpec((tm, d), row),
        out_shape=jax.ShapeDtypeStruct((n, d), F32),
        compiler_params=pltpu.CompilerParams(
            dimension_semantics=("arbitrary",), vmem_limit_bytes=VMEM_LIMIT),
        name="merge_sample" if sample else "merge_prompt",
    )(x2d, ro, rg, fo, cu, cv, seg, retg, sw, sb, wout)


def _ffn_begin(x_ref, g2_ref, h_sc, acc_sc):
    @pl.when(pl.program_id(1) == 0)
    def _():
        h_sc[...] = _row_rms(x_ref[...], g2_ref[...]).astype(BF16)
        acc_sc[...] = jnp.zeros_like(acc_sc)


def _ffn_finish(x_ref, out_ref, acc_sc, conv_g, conv_v, wd_ref):
    act = (jax.nn.silu(conv_g) * conv_v).astype(BF16)
    acc_sc[...] += _dot(act, wd_ref[...])

    @pl.when(pl.program_id(1) == pl.num_programs(1) - 1)
    def _():
        out_ref[...] = x_ref[...] + acc_sc[...]


def _ffn_prompt_kernel(x_ref, g2_ref, wug_ref, wuv_ref, cwg_ref, cwv_ref, cbg_ref, cbv_ref, wd_ref,
                       pg_ref, pv_ref, out_ref, st_ref, h_sc, acc_sc, cg_sc, cv_sc,
                       *, tiles_per_seq, d_ff):
    i = pl.program_id(0)
    j = pl.program_id(1)
    b = i // tiles_per_seq
    t = i % tiles_per_seq
    tm = x_ref.shape[0]
    fb = wug_ref.shape[1]
    _ffn_begin(x_ref, g2_ref, h_sc, acc_sc)

    @pl.when(t == 0)
    def _():
        cg_sc[j] = pg_ref[...]
        cv_sc[j] = pv_ref[...]

    hb = h_sc[...]
    row = lax.broadcasted_iota(jnp.int32, (tm, fb), 0)

    def conv(u, prev, w_ref, b_ref):
        s1 = jnp.where(row == 0, prev[1:2], pltpu.roll(u, 1, 0))
        s2 = jnp.where(row == 0, prev[0:1], jnp.where(row == 1, prev[1:2], pltpu.roll(u, 2, 0)))
        return ((b_ref[...] + w_ref[0:1] * s2) + w_ref[1:2] * s1) + w_ref[2:3] * u

    ug = _dot(hb, wug_ref[...])
    uv = _dot(hb, wuv_ref[...])
    conv_g = conv(ug, cg_sc[j], cwg_ref, cbg_ref)
    conv_v = conv(uv, cv_sc[j], cwv_ref, cbv_ref)
    last_g = ug[tm - (CONV_W - 1):tm, :]
    last_v = uv[tm - (CONV_W - 1):tm, :]
    cg_sc[j] = last_g
    cv_sc[j] = last_v

    @pl.when(t == tiles_per_seq - 1)
    def _():
        c0 = pl.multiple_of(j * fb, LANES)
        st_ref[b, :, pl.ds(c0, fb)] = last_g
        st_ref[b, :, pl.ds(pl.multiple_of(d_ff + j * fb, LANES), fb)] = last_v

    _ffn_finish(x_ref, out_ref, acc_sc, conv_g, conv_v, wd_ref)


def _ffn_sample_kernel(x_ref, g2_ref, wug_ref, wuv_ref, cwg_ref, cwv_ref, cbg_ref, cbv_ref, wd_ref,
                       p0g_ref, p0v_ref, p1g_ref, p1v_ref, out_ref, st_ref, h_sc, acc_sc, *, d_ff):
    j = pl.program_id(1)
    fb = wug_ref.shape[1]
    _ffn_begin(x_ref, g2_ref, h_sc, acc_sc)
    hb = h_sc[...]

    def conv(u, p0, p1, w_ref, b_ref):
        return ((b_ref[...] + w_ref[0:1] * p0) + w_ref[1:2] * p1) + w_ref[2:3] * u

    ug = _dot(hb, wug_ref[...])
    uv = _dot(hb, wuv_ref[...])
    p1g = p1g_ref[...]
    p1v = p1v_ref[...]
    conv_g = conv(ug, p0g_ref[...], p1g, cwg_ref, cbg_ref)
    conv_v = conv(uv, p0v_ref[...], p1v, cwv_ref, cbv_ref)
    cg = pl.multiple_of(j * fb, LANES)
    cv = pl.multiple_of(d_ff + j * fb, LANES)
    st_ref[:, pl.ds(cg, fb)] = p1g
    st_ref[:, pl.ds(cv, fb)] = p1v
    st_ref[:, pl.ds(pl.multiple_of(2 * d_ff + j * fb, LANES), fb)] = ug
    st_ref[:, pl.ds(pl.multiple_of(3 * d_ff + j * fb, LANES), fb)] = uv
    _ffn_finish(x_ref, out_ref, acc_sc, conv_g, conv_v, wd_ref)


def _ffn(x2d, g2, wup, cw, cb, wdown, prev, tm, fb, seq):
    n, d = x2d.shape
    d_ff = wdown.shape[0]
    nff = d_ff // fb
    nt = n // tm
    xspec = pl.BlockSpec((tm, d), lambda i, j: (i, 0))
    common = [xspec,
              pl.BlockSpec((1, d), lambda i, j: (0, 0)),
              pl.BlockSpec((d, fb), lambda i, j: (0, j)),
              pl.BlockSpec((d, fb), lambda i, j: (0, nff + j)),
              pl.BlockSpec((CONV_W, fb), lambda i, j: (0, j)),
              pl.BlockSpec((CONV_W, fb), lambda i, j: (0, nff + j)),
              pl.BlockSpec((1, fb), lambda i, j: (0, j)),
              pl.BlockSpec((1, fb), lambda i, j: (0, nff + j)),
              pl.BlockSpec((fb, d), lambda i, j: (j, 0))]
    scratch = [pltpu.VMEM((tm, d), BF16), pltpu.VMEM((tm, d), F32)]
    params = pltpu.CompilerParams(
        dimension_semantics=("arbitrary", "arbitrary"), vmem_limit_bytes=VMEM_LIMIT)
    if seq == 1:
        bsz = n
        prev2 = prev.reshape(bsz, (CONV_W - 1) * 2 * d_ff)
        pspecs = [pl.BlockSpec((bsz, fb), lambda i, j, o=o: (0, o + j))
                  for o in (0, nff, 2 * nff, 3 * nff)]
        out, st = pl.pallas_call(
            functools.partial(_ffn_sample_kernel, d_ff=d_ff),
            grid=(nt, nff),
            in_specs=common + [pspecs[0], pspecs[1], pspecs[2], pspecs[3]],
            out_specs=[xspec, pl.BlockSpec(prev2.shape, lambda i, j: (0, 0))],
            out_shape=[jax.ShapeDtypeStruct((n, d), F32), jax.ShapeDtypeStruct(prev2.shape, F32)],
            scratch_shapes=scratch,
            compiler_params=params,
            name="ffn_sample",
        )(x2d, g2, wup, wup, cw, cw, cb, cb, wdown, prev2, prev2, prev2, prev2)
        return out, st.reshape(prev.shape)
    bsz = n // seq
    tiles_per_seq = seq // tm
    pspec_g = pl.BlockSpec((None, CONV_W - 1, fb), lambda i, j: (i // tiles_per_seq, 0, j))
    pspec_v = pl.BlockSpec((None, CONV_W - 1, fb), lambda i, j: (i // tiles_per_seq, 0, nff + j))
    carry = pltpu.VMEM((nff, CONV_W - 1, fb), F32)
    return pl.pallas_call(
        functools.partial(_ffn_prompt_kernel, tiles_per_seq=tiles_per_seq, d_ff=d_ff),
        grid=(nt, nff),
        in_specs=common + [pspec_g, pspec_v],
        out_specs=[xspec, pl.BlockSpec(prev.shape, lambda i, j: (0, 0, 0))],
        out_shape=[jax.ShapeDtypeStruct((n, d), F32), jax.ShapeDtypeStruct(prev.shape, F32)],
        scratch_shapes=scratch + [carry, carry],
        compiler_params=params,
        name="ffn_prompt",
    )(x2d, g2, wup, wup, cw, cw, cb, cb, wdown, prev, prev)


def _rope_tables(pos):
    half = HEAD_DIM // 2
    inv = 1.0 / (ROPE_BASE ** (jnp.arange(half, dtype=F32) / half))
    ang = pos.astype(F32)[:, None] * inv[None, :]
    cos = jnp.cos(ang)
    sin = jnp.sin(ang)
    cos_e = jnp.tile(jnp.concatenate([cos, cos], axis=-1), (1, H_RET))
    sin_e = jnp.tile(jnp.concatenate([-sin, sin], axis=-1), (1, H_RET))
    return cos_e, sin_e


def _retention_tables():
    c = CHUNK
    lg = jnp.log1p(-jnp.exp2(-5.0 - jnp.arange(H_RET, dtype=F32)))
    idx = jnp.arange(c, dtype=F32)
    diff = idx[:, None] - idx[None, :]
    intra = jnp.where(diff[None] >= 0, jnp.exp(jnp.maximum(diff, 0.0)[None] * lg[:, None, None]), 0.0)
    q_dec = jnp.exp((idx[None, :] + 1.0) * lg[:, None])
    k_dec = jnp.exp((c - 1.0 - idx[None, :]) * lg[:, None])
    chunk_dec = jnp.exp(c * lg)
    expand = lambda a: jnp.repeat(a.T, HEAD_DIM, axis=1)
    head_of = jnp.arange(RET_W) // HEAD_DIM
    bd = (head_of[:, None] == head_of[None, :]).astype(F32)
    decmat = jnp.broadcast_to(chunk_dec[head_of][:, None], (RET_W, RET_W))
    gamma = jnp.exp(1.0 * lg)
    gcol = jnp.broadcast_to(gamma[:, None, None], (H_RET, HEAD_DIM, 1))
    grow = jnp.broadcast_to(gamma[:, None, None], (H_RET, 1, HEAD_DIM))
    return dict(intra=intra, qdec=expand(q_dec), kdec=expand(k_dec), decmat=decmat, bd=bd,
                gcol=gcol, grow=grow)


def _pick_tile(n, target):
    t = min(n, target)
    while n % t:
        t //= 2
    return t


@jax.jit
def kernel(x_prompt, x_sample, cache_k, cache_v, cache_logf, state_ret, state_ffn_conv, page_table,
           norm1_g, w_in, b_forget, ret_norm_g, q_norm_g, k_norm_g, spatial_w, spatial_b, w_out,
           norm2_g, w_up, conv_w, conv_b, w_down):
    bp, lp, d = x_prompt.shape
    bs, ls, _ = x_sample.shape
    depth = w_in.shape[0]
    d_ff = w_down.shape[1]
    n_pages = page_table.shape[1]
    past_len = n_pages * PAGE_SIZE
    assert ls == 1 and lp % CHUNK == 0

    fz0 = 4 * RET_W + 3 * FOX_W
    w_in_p = jnp.concatenate(
        [w_in[..., :fz0], w_in[..., fz0 + H_FOX:], w_in[..., fz0:fz0 + H_FOX],
         jnp.zeros((depth, d, FZ_COLS - H_FOX), w_in.dtype)], axis=-1).astype(BF16)
    w_out_b = w_out.astype(BF16)
    w_up_b = w_up.astype(BF16)
    w_down_b = w_down.astype(BF16)
    bf_p = jnp.pad(b_forget, ((0, 0), (0, FZ_COLS - H_FOX)))
    qg_e = jnp.tile(q_norm_g, (1, H_FOX))
    kg_e = jnp.tile(k_norm_g, (1, H_FOX))
    retg_e = ret_norm_g.reshape(depth, RET_W)
    sb_e = jnp.repeat(jnp.swapaxes(spatial_b, 1, 2), HEAD_DIM, axis=2)
    sw0_e = jnp.repeat(spatial_w[:, :, 0, 0], HEAD_DIM, axis=1)
    sb0_e = jnp.repeat(spatial_b[:, :, 0], HEAD_DIM, axis=1)

    head_of = jnp.arange(256) // HEAD_DIM
    seg = (head_of[:, None] == head_of[None, :]).astype(BF16)
    idx = jnp.arange(CHUNK)
    tri_incl = (idx[:, None] >= idx[None, :]).astype(BF16)
    sfx = (idx[:, None] > idx[None, :]).astype(BF16)
    w_sfx = jnp.concatenate([sfx, jnp.ones((PAGE_SIZE, PAGE_SIZE), BF16)], axis=1)
    ck_t = jnp.transpose(cache_k, (0, 1, 3, 4, 2)).reshape(depth, -1, FOX_W, PAGE_SIZE)
    cv_t = jnp.transpose(cache_v, (0, 1, 3, 4, 2)).reshape(depth, -1, FOX_W, PAGE_SIZE)
    clf_t = jnp.transpose(cache_logf, (0, 1, 3, 2))
    cos_p, sin_p = _rope_tables(jnp.arange(lp))
    cos_s, sin_s = _rope_tables(jnp.broadcast_to(past_len + jnp.arange(ls), (bs,)))
    tabs = _retention_tables()

    tm_p = _pick_tile(lp, 512)
    tq = _pick_tile(lp, 256)
    tm_f = _pick_tile(lp, 1024)
    fb = 256
    g_pages = _pick_tile(n_pages, 8)
    zeros_prev = jnp.zeros((bp, CONV_W - 1, 2 * d_ff), F32)

    xp = x_prompt.reshape(bp * lp, d)
    xs = x_sample.reshape(bs * ls, d)
    outs = [[] for _ in range(11)]
    for l in range(depth):
        g1 = norm1_g[l][None]
        g2 = norm2_g[l][None]
        bfl = bf_p[l][None]
        qg = qg_e[l][None]
        kg = kg_e[l][None]
        retg = retg_e[l][None]
        cb = conv_b[l][None]

        rq, rk, rv, rg, fq, fk, fv, lf, lfp, cu, cv = _proj(
            xp, cos_p, sin_p, g1, w_in_p[l], bfl, qg, kg, seg, tm_p)
        ro, r_state = _ret_prompt(rq, rk, rv, tabs, bp, lp)
        cum, cumt = _cum(lfp, tri_incl, bp, lp)
        fo = _fox_prompt(fq, fk, fv, cum, cumt, bp, lp, tq, tq)
        xp = _merge(xp, ro, rg, fo, cu, cv, seg, retg, spatial_w[l], sb_e[l], w_out_b[l], tm_p, False)
        xp, c_state = _ffn(xp, g2, w_up_b[l], conv_w[l], cb, w_down_b[l], zeros_prev, tm_f, fb, lp)
        outs[0].append(fk.reshape(bp, lp, H_FOX, HEAD_DIM))
        outs[1].append(fv.reshape(bp, lp, H_FOX, HEAD_DIM))
        outs[2].append(lf.reshape(bp, lp, H_FOX))
        outs[3].append(r_state)
        outs[4].append(c_state)

        rq, rk, rv, rg, fq, fk, fv, lf, lfp, cu, cv = _proj(
            xs, cos_s, sin_s, g1, w_in_p[l], bfl, qg, kg, seg, bs)
        ro, r_state = _ret_sample(rq, rk, rv, state_ret, l, tabs)
        fo = _fox_sample(fq, fk, fv, lf, ck_t, cv_t, clf_t, page_table, l, w_sfx, g_pages)
        xs = _merge(xs, ro, rg, fo, cu, cv, seg, retg, sw0_e[l][None], sb0_e[l][None], w_out_b[l],
                    bs, True)
        xs, c_state = _ffn(xs, g2, w_up_b[l], conv_w[l], cb, w_down_b[l], state_ffn_conv[l], bs, fb, 1)
        outs[5].append(fk.reshape(bs, ls, H_FOX, HEAD_DIM))
        outs[6].append(fv.reshape(bs, ls, H_FOX, HEAD_DIM))
        outs[7].append(lf.reshape(bs, ls, H_FOX))
        outs[8].append(r_state)
        outs[9].append(cv.reshape(bs, ls, CMLP_W))
        outs[10].append(c_state)

    stacked = [jnp.stack(o) for o in outs]
    return (xp.reshape(bp, lp, d), xs.reshape(bs, ls, d), *stacked)
```

```python
import functools

import jax
import jax.numpy as jnp
from jax import lax
from jax.experimental import pallas as pl
from jax.experimental.pallas import tpu as pltpu

HEAD_DIM = 64
H_RET = 4
H_FOX = 8
C_GROUPS = 4
RET_W = H_RET * HEAD_DIM
FOX_W = H_FOX * HEAD_DIM
CMLP_W = C_GROUPS * HEAD_DIM
CHUNK = 128
PAGE_SIZE = 128
CONV_W = 3
RMS_EPS = 1e-6
ROPE_BASE = 10000.0
ATTN_SCALE = HEAD_DIM ** -0.5

LANES = 128
FZ_COLS = LANES
VMEM_LIMIT = 56 * 1024 * 1024
NEG_BIG = -1e30
LOG2E = 1.4426950408889634

F32 = jnp.float32
BF16 = jnp.bfloat16


def _dot(a, b):
    return jnp.dot(a, b, preferred_element_type=F32)


def _dot_nt(a, b):
    return lax.dot_general(a, b, (((1,), (1,)), ((), ())), preferred_element_type=F32)


def _split2(x):
    hi = x.astype(BF16)
    lo = (x - hi.astype(F32)).astype(BF16)
    return hi, lo


def _split3(x):
    a = x.astype(BF16)
    r = x - a.astype(F32)
    b = r.astype(BF16)
    c = (r - b.astype(F32)).astype(BF16)
    return a, b, c


def _seg_mean(sq, seg):
    hi, lo = _split2(sq)
    return (_dot(hi, seg) + _dot(lo, seg)) * (1.0 / HEAD_DIM)


def _head_rms(x, seg, g):
    parts = []
    for c in range(x.shape[-1] // 256):
        xc = x[:, c * 256:(c + 1) * 256]
        parts.append(xc * lax.rsqrt(_seg_mean(xc * xc, seg) + RMS_EPS))
    y = parts[0] if len(parts) == 1 else jnp.concatenate(parts, axis=-1)
    return y * g


def _row_rms(x, g):
    return x * lax.rsqrt(jnp.mean(x * x, axis=-1, keepdims=True) + RMS_EPS) * g


def _proj_kernel(x_ref, cos_ref, sin_ref, g1_ref, w_ref, bf_ref, qg_ref, kg_ref, seg_ref,
                 rq_ref, rk_ref, rv_ref, rg_ref, fq_ref, fk_ref, fv_ref, lf_ref, lfp_ref,
                 cu_ref, cv_ref, *, kv_transposed):
    hb = _row_rms(x_ref[...], g1_ref[...]).astype(BF16)
    seg = seg_ref[...]

    def proj(lo, n):
        return _dot(hb, w_ref[:, lo:lo + n])

    cos = cos_ref[...]
    sin = sin_ref[...]
    lane = lax.broadcasted_iota(jnp.int32, cos.shape, 1)
    first_half = (lane % HEAD_DIM) < (HEAD_DIM // 2)

    def rot(z):
        swapped = jnp.where(first_half, pltpu.roll(z, RET_W - HEAD_DIM // 2, 1),
                            pltpu.roll(z, HEAD_DIM // 2, 1))
        return z * cos + swapped * sin

    rq_ref[...] = rot(proj(0, RET_W))
    rk_ref[...] = rot(proj(RET_W, RET_W)) * ATTN_SCALE
    rv_ref[...] = proj(2 * RET_W, RET_W)
    rg_ref[...] = proj(3 * RET_W, RET_W)
    off = 4 * RET_W
    fq_ref[...] = _head_rms(proj(off, FOX_W), seg, qg_ref[...])
    fk = _head_rms(proj(off + FOX_W, FOX_W), seg, kg_ref[...])
    fv = proj(off + 2 * FOX_W, FOX_W)
    fk_ref[...] = fk.T if kv_transposed else fk
    fv_ref[...] = fv.T if kv_transposed else fv
    off += 3 * FOX_W
    cu_ref[...] = jax.nn.gelu(proj(off, CMLP_W))
    cv_ref[...] = jax.nn.gelu(proj(off + CMLP_W, CMLP_W))
    off += 2 * CMLP_W
    lf = jax.nn.log_sigmoid(proj(off, FZ_COLS) + bf_ref[...])
    lane_z = lax.broadcasted_iota(jnp.int32, lf.shape, 1)
    lfp_ref[...] = jnp.where(lane_z < H_FOX, lf, 0.0)
    lf_ref[...] = lf[:, :H_FOX]


def _proj(x2d, cos, sin, g1, w, bfp, qg, kg, seg, tm, kv_transposed):
    n, d = x2d.shape
    nt = n // tm
    tt = cos.shape[0] // tm
    row = lambda i: (i, 0)
    const = lambda i: (0, 0)
    outs = [RET_W] * 4 + [FOX_W] * 3 + [H_FOX, FZ_COLS] + [CMLP_W] * 2
    out_specs = [pl.BlockSpec((tm, c), row) for c in outs]
    out_shape = [jax.ShapeDtypeStruct((n, c), F32) for c in outs]
    if kv_transposed:
        for o in (5, 6):
            out_specs[o] = pl.BlockSpec((None, FOX_W, tm), lambda i: (i // tt, 0, i % tt))
            out_shape[o] = jax.ShapeDtypeStruct((nt // tt, FOX_W, tt * tm), F32)
    return pl.pallas_call(
        functools.partial(_proj_kernel, kv_transposed=kv_transposed),
        grid=(nt,),
        in_specs=[
            pl.BlockSpec((tm, d), row),
            pl.BlockSpec((tm, RET_W), lambda i: (i % tt, 0)),
            pl.BlockSpec((tm, RET_W), lambda i: (i % tt, 0)),
            pl.BlockSpec((1, d), const),
            pl.BlockSpec(w.shape, const),
            pl.BlockSpec((1, FZ_COLS), const),
            pl.BlockSpec((1, FOX_W), const),
            pl.BlockSpec((1, FOX_W), const),
            pl.BlockSpec((256, 256), const),
        ],
        out_specs=out_specs,
        out_shape=out_shape,
        compiler_params=pltpu.CompilerParams(
            dimension_semantics=("arbitrary",), vmem_limit_bytes=VMEM_LIMIT),
        name="proj",
    )(x2d, cos, sin, g1, w, bfp, qg, kg, seg)


def _ret_prompt_kernel(q_ref, k_ref, v_ref, intra_ref, qdec_ref, kdec_ref, dec_ref, bd_ref,
                       o_ref, st_ref, s_sc):
    c = pl.program_id(1)

    @pl.when(c == 0)
    def _():
        s_sc[...] = jnp.zeros_like(s_sc)

    q = q_ref[...]
    k = k_ref[...]
    vb = v_ref[...].astype(BF16)
    kb = k.astype(BF16)
    lane = lax.broadcasted_iota(jnp.int32, q.shape, 1)
    s_old = s_sc[...]
    o = _dot((q * qdec_ref[...]).astype(BF16), s_old.astype(BF16))
    for h in range(H_RET):
        hm = (lane // HEAD_DIM) == h
        sc = _dot_nt(jnp.where(hm, q, 0.0).astype(BF16), kb) * intra_ref[h]
        o = o + jnp.where(hm, _dot(sc.astype(BF16), vb), 0.0)
    o_ref[...] = o
    kd = (k * kdec_ref[...]).astype(BF16)
    ktv = lax.dot_general(kd, vb, (((0,), (0,)), ((), ())), preferred_element_type=F32)
    s_new = s_old * dec_ref[...] + bd_ref[...] * ktv
    s_sc[...] = s_new

    @pl.when(c == pl.num_programs(1) - 1)
    def _():
        for h in range(H_RET):
            lo = h * HEAD_DIM
            st_ref[h] = s_new[lo:lo + HEAD_DIM, lo:lo + HEAD_DIM]


def _ret_prompt(rq, rk, rv, tabs, bsz, seq):
    nc = seq // CHUNK
    blk = pl.BlockSpec((CHUNK, RET_W), lambda b, c: (b * nc + c, 0))
    c2 = lambda b, c: (0, 0)
    return pl.pallas_call(
        _ret_prompt_kernel,
        grid=(bsz, nc),
        in_specs=[blk, blk, blk,
                  pl.BlockSpec((H_RET, CHUNK, CHUNK), lambda b, c: (0, 0, 0)),
                  pl.BlockSpec((CHUNK, RET_W), c2),
                  pl.BlockSpec((CHUNK, RET_W), c2),
                  pl.BlockSpec((RET_W, RET_W), c2),
                  pl.BlockSpec((RET_W, RET_W), c2)],
        out_specs=[blk, pl.BlockSpec((None, H_RET, HEAD_DIM, HEAD_DIM), lambda b, c: (b, 0, 0, 0))],
        out_shape=[jax.ShapeDtypeStruct((bsz * seq, RET_W), F32),
                   jax.ShapeDtypeStruct((bsz, H_RET, HEAD_DIM, HEAD_DIM), F32)],
        scratch_shapes=[pltpu.VMEM((RET_W, RET_W), F32)],
        compiler_params=pltpu.CompilerParams(
            dimension_semantics=("arbitrary", "arbitrary"), vmem_limit_bytes=VMEM_LIMIT),
        name="ret_prompt",
    )(rq, rk, rv, tabs["intra"], tabs["qdec"], tabs["kdec"], tabs["decmat"], tabs["bd"])


def _ret_sample_kernel(q_ref, k_ref, v_ref, s_ref, gcol_ref, grow_ref, o_ref, st_ref):
    for h in range(H_RET):
        qc = q_ref[h]
        kc = k_ref[h]
        vr = v_ref[h]
        s = s_ref[h]
        qk = jnp.sum(qc * kc, axis=0, keepdims=True)
        inter = jnp.sum((qc * gcol_ref[h]) * s, axis=0, keepdims=True)
        o_ref[h] = qk * vr + inter
        st_ref[h] = s * grow_ref[h] + kc * vr


def _ret_sample(rq, rk, rv, state, layer, tabs):
    bsz = rq.shape[0]
    qc = rq.reshape(bsz, H_RET, HEAD_DIM, 1)
    kc = rk.reshape(bsz, H_RET, HEAD_DIM, 1)
    vr = rv.reshape(bsz, H_RET, 1, HEAD_DIM)
    col = pl.BlockSpec((None, H_RET, HEAD_DIM, 1), lambda b: (b, 0, 0, 0))
    rowb = pl.BlockSpec((None, H_RET, 1, HEAD_DIM), lambda b: (b, 0, 0, 0))
    stb = pl.BlockSpec((None, H_RET, HEAD_DIM, HEAD_DIM), lambda b: (b, 0, 0, 0))
    o, st = pl.pallas_call(
        _ret_sample_kernel,
        grid=(bsz,),
        in_specs=[col, col, rowb,
                  pl.BlockSpec((None, None, H_RET, HEAD_DIM, HEAD_DIM), lambda b: (layer, b, 0, 0, 0)),
                  pl.BlockSpec((H_RET, HEAD_DIM, 1), lambda b: (0, 0, 0)),
                  pl.BlockSpec((H_RET, 1, HEAD_DIM), lambda b: (0, 0, 0))],
        out_specs=[rowb, stb],
        out_shape=[jax.ShapeDtypeStruct((bsz, H_RET, 1, HEAD_DIM), F32),
                   jax.ShapeDtypeStruct((bsz, H_RET, HEAD_DIM, HEAD_DIM), F32)],
        compiler_params=pltpu.CompilerParams(dimension_semantics=("arbitrary",)),
        name="ret_sample",
    )(qc, kc, vr, state, tabs["gcol"], tabs["grow"])
    return o.reshape(bsz, RET_W), st


def _cum_kernel(lf_ref, tri_ref, cum_ref, cumt_ref, lft_ref):
    tri = tri_ref[...]
    carry = jnp.zeros((1, LANES), F32)
    for c in range(lf_ref.shape[0] // CHUNK):
        sl = slice(c * CHUNK, (c + 1) * CHUNK)
        x = lf_ref[sl, :]
        a, b, r = _split3(x)
        cc = (_dot(tri, a) + _dot(tri, b)) + _dot(tri, r) + carry
        cum_ref[sl, :] = cc
        cumt_ref[:, sl] = cc.T[:H_FOX, :]
        lft_ref[:, sl] = x.T[:H_FOX, :]
        carry = cc[CHUNK - 1:CHUNK, :]


def _cum(lfp, tri, bsz, seq):
    col = pl.BlockSpec((None, H_FOX, seq), lambda b: (b, 0, 0))
    return pl.pallas_call(
        _cum_kernel,
        grid=(bsz,),
        in_specs=[pl.BlockSpec((seq, LANES), lambda b: (b, 0)),
                  pl.BlockSpec((CHUNK, CHUNK), lambda b: (0, 0))],
        out_specs=[pl.BlockSpec((seq, LANES), lambda b: (b, 0)), col, col],
        out_shape=[jax.ShapeDtypeStruct((bsz * seq, LANES), F32),
                   jax.ShapeDtypeStruct((bsz, H_FOX, seq), F32),
                   jax.ShapeDtypeStruct((bsz, H_FOX, seq), F32)],
        compiler_params=pltpu.CompilerParams(dimension_semantics=("arbitrary",)),
        name="cum",
    )(lfp, tri)


def _fox_prompt_kernel(q_ref, kt_ref, vt_ref, cum_ref, cumt_ref, o_ref, k_sc, vt_sc, ck_sc,
                       s_sc, p_sc, *, tq, tk, prep_chunk):
    hp = pl.program_id(1)
    i = pl.program_id(2)
    seq = kt_ref.shape[1]
    last = (i + 1) * (tq // tk) - 1

    @pl.when(i == 0)
    def _():
        lane = lax.broadcasted_iota(jnp.int32, (prep_chunk, LANES), 1)
        for c in range(seq // prep_chunk):
            sl = slice(c * prep_chunk, (c + 1) * prep_chunk)
            k_sc[sl, :] = kt_ref[:, sl].T.astype(BF16)
            vt_sc[:, sl] = vt_ref[:, sl].astype(BF16)
            cum = cum_ref[sl, :]
            for hh in range(2):
                col = jnp.sum(jnp.where(lane == 2 * hp + hh, cum, 0.0), axis=-1, keepdims=True)
                ck_sc[hh, sl, :] = jnp.broadcast_to(col * LOG2E, (prep_chunk, LANES))

    qt = (q_ref[...] * (ATTN_SCALE * LOG2E)).T
    row_head = lax.broadcasted_iota(jnp.int32, qt.shape, 0) // HEAD_DIM
    qts = [jnp.where(row_head == hh, qt, 0.0).astype(BF16) for hh in range(2)]
    q0 = pl.multiple_of(i * tq, tq)
    cqs = [cumt_ref[pl.ds(2 * hp + hh, 1), pl.ds(q0, tq)] * LOG2E for hh in range(2)]

    def scores(j):
        k0 = pl.multiple_of(j * tk, tk)
        kb = k_sc[pl.ds(k0, tk), :]
        out = []
        for hh in range(2):
            ck = ck_sc[hh, pl.ds(k0, tk), :]
            out.append(_dot(kb, qts[hh]) + (cqs[hh] - jnp.concatenate([ck] * (tq // LANES), axis=1)))
        return out

    def weighted_values(hh, j, p):
        k0 = pl.multiple_of(j * tk, tk)
        return _dot(vt_sc[hh * HEAD_DIM:(hh + 1) * HEAD_DIM, pl.ds(k0, tk)], p)

    def body(j, carry, masked):
        cur = lax.rem(j, 2)
        nxt = 1 - cur
        s_next = scores(jnp.minimum(j + 1, last))
        if masked:
            keep = (lax.broadcasted_iota(jnp.int32, (tk, tq), 0) + j * tk
                    <= lax.broadcasted_iota(jnp.int32, (tk, tq), 1) + i * tq)
        new = []
        probs = []
        for hh in range(2):
            m, l, acc, a_prev = carry[hh]
            acc = a_prev * acc + weighted_values(hh, jnp.maximum(j - 1, 0), p_sc[cur, hh])
            s = s_sc[cur, hh]
            if masked:
                s = jnp.where(keep, s, -jnp.inf)
            m_new = jnp.maximum(m, jnp.max(s, axis=0, keepdims=True))
            alpha = jnp.exp2(m - m_new)
            p = jnp.exp2(s - m_new)
            l = alpha * l + jnp.sum(p, axis=0, keepdims=True)
            probs.append(p.astype(BF16))
            new.append((m_new, l, acc, alpha))
        for hh in range(2):
            p_sc[nxt, hh] = probs[hh]
            s_sc[nxt, hh] = s_next[hh]
        return tuple(new)

    s_first = scores(0)
    for hh in range(2):
        s_sc[0, hh] = s_first[hh]
        p_sc[0, hh] = jnp.zeros((tk, tq), BF16)
    carry = tuple((jnp.full((1, tq), NEG_BIG, F32), jnp.zeros((1, tq), F32),
                   jnp.zeros((HEAD_DIM, tq), F32), jnp.ones((1, tq), F32)) for _ in range(2))
    n_diag = tq // tk
    carry = lax.fori_loop(0, last + 1 - n_diag, lambda j, c: body(j, c, False), carry)
    for t in range(n_diag):
        carry = body(last + 1 - n_diag + t, carry, True)
    fin = lax.rem(last + 1, 2)
    outs = []
    for hh in range(2):
        _, l, acc, a_prev = carry[hh]
        outs.append((a_prev * acc + weighted_values(hh, last, p_sc[fin, hh])) / l)
    o_ref[...] = jnp.concatenate(outs, axis=0).T


def _fox_prompt(fq, fkt, fvt, cum, cumt, bsz, seq, tq, tk):
    assert tq % tk == 0
    nq = seq // tq
    npair = H_FOX // 2
    prep_chunk = _pick_tile(seq, 512)
    kv = pl.BlockSpec((None, LANES, seq), lambda b, h, i: (b, h, 0))
    return pl.pallas_call(
        functools.partial(_fox_prompt_kernel, tq=tq, tk=tk, prep_chunk=prep_chunk),
        grid=(bsz, npair, nq),
        in_specs=[pl.BlockSpec((tq, LANES), lambda b, h, i: (b * nq + i, h)),
                  kv, kv,
                  pl.BlockSpec((seq, LANES), lambda b, h, i: (b, 0)),
                  pl.BlockSpec((None, H_FOX, seq), lambda b, h, i: (b, 0, 0))],
        out_specs=pl.BlockSpec((tq, LANES), lambda b, h, i: (b * nq + i, h)),
        out_shape=jax.ShapeDtypeStruct((bsz * seq, FOX_W), F32),
        scratch_shapes=[pltpu.VMEM((seq, LANES), BF16), pltpu.VMEM((LANES, seq), BF16),
                        pltpu.VMEM((2, seq, LANES), F32),
                        pltpu.VMEM((2, 2, tk, tq), F32), pltpu.VMEM((2, 2, tk, tq), BF16)],
        compiler_params=pltpu.CompilerParams(
            dimension_semantics=("arbitrary", "arbitrary", "arbitrary"),
            vmem_limit_bytes=VMEM_LIMIT),
        name="fox_prompt",
    )(fq, fkt, fvt, cum, cumt)


def _fox_sample_kernel(pt_ref, q_ref, kn_ref, vn_ref, lfn_ref, w_ref, *rest, pages_per_step):
    del pt_ref
    g = pages_per_step
    k_refs = rest[:g]
    v_refs = rest[g:2 * g]
    lf_refs = rest[2 * g:3 * g]
    o_ref = rest[3 * g]
    m_sc, l_sc, acc_sc, after_sc = rest[3 * g + 1:]
    jg = pl.program_id(1)

    sub = lax.broadcasted_iota(jnp.int32, (H_FOX, FOX_W), 0)
    lane = lax.broadcasted_iota(jnp.int32, (H_FOX, FOX_W), 1)
    hmask = (lane // HEAD_DIM) == sub
    qrows = jnp.where(hmask, q_ref[...] * ATTN_SCALE, 0.0)

    @pl.when(jg == 0)
    def _():
        m_sc[...] = jnp.full_like(m_sc, NEG_BIG)
        l_sc[...] = jnp.zeros_like(l_sc)
        acc_sc[...] = jnp.zeros_like(acc_sc)
        after_sc[...] = jnp.zeros_like(after_sc)

    qb = qrows.astype(BF16)
    w = w_ref[...]
    m = m_sc[...]
    l = l_sc[...]
    after = after_sc[...]
    lf_all = jnp.concatenate([r[...] for r in lf_refs], axis=0)
    a, b, c = _split3(lf_all)
    st_all = (_dot(a, w) + _dot(b, w)) + _dot(c, w)
    scores = []
    for t in range(g):
        st = st_all[t * H_FOX:(t + 1) * H_FOX]
        kb = k_refs[t][...].astype(BF16)
        scores.append(_dot(qb, kb) + st[:, :PAGE_SIZE] + after)
        after = after + st[:, PAGE_SIZE:]
    s_max = scores[0]
    for t in range(1, g):
        s_max = jnp.maximum(s_max, scores[t])
    m_new = jnp.maximum(m, jnp.max(s_max, axis=-1, keepdims=True))
    alpha = jnp.exp(m - m_new)
    p_sum = None
    pv = None
    for t in range(g):
        p = jnp.exp(scores[t] - m_new)
        p_sum = p if p_sum is None else p_sum + p
        d = _dot_nt(p.astype(BF16), v_refs[t][...].astype(BF16))
        pv = d if pv is None else pv + d
    l = alpha * l + jnp.sum(p_sum, axis=-1, keepdims=True)
    acc = alpha * acc_sc[...] + pv
    m = m_new
    m_sc[...] = m
    l_sc[...] = l
    acc_sc[...] = acc
    after_sc[...] = after

    @pl.when(jg == pl.num_programs(1) - 1)
    def _():
        s_new = jnp.sum(qrows * kn_ref[...], axis=-1, keepdims=True) - lfn_ref[...]
        m_fin = jnp.maximum(m, s_new)
        alpha = jnp.exp(m - m_fin)
        p_new = jnp.exp(s_new - m_fin)
        l_fin = alpha * l + p_new
        acc_fin = alpha * acc + p_new * vn_ref[...]
        out = jnp.where(hmask, acc_fin / l_fin, 0.0)
        o_ref[...] = jnp.sum(out, axis=0, keepdims=True)


def _fox_sample(fq, fk, fv, lf, ck_t, cv_t, clf_t, page_table, layer, w_sfx, g):
    bsz = fq.shape[0]
    n_pages = page_table.shape[1]
    n_steps = n_pages // g
    q3 = fq.reshape(bsz, 1, FOX_W)
    k3 = fk.reshape(bsz, 1, FOX_W)
    v3 = fv.reshape(bsz, 1, FOX_W)
    lf3 = lf.reshape(bsz, H_FOX, 1)
    row = pl.BlockSpec((None, 1, FOX_W), lambda b, j, pt: (b, 0, 0))

    def page_spec(rows, t):
        def imap(b, j, pt):
            return (layer, pt[b, n_pages - 1 - (j * g + t)], 0, 0)
        return pl.BlockSpec((None, None, rows, PAGE_SIZE), imap)

    in_specs = [row, row, row,
                pl.BlockSpec((None, H_FOX, 1), lambda b, j, pt: (b, 0, 0)),
                pl.BlockSpec(w_sfx.shape, lambda b, j, pt: (0, 0))]
    in_specs += [page_spec(FOX_W, t) for t in range(g)]
    in_specs += [page_spec(FOX_W, t) for t in range(g)]
    in_specs += [page_spec(H_FOX, t) for t in range(g)]
    out = pl.pallas_call(
        functools.partial(_fox_sample_kernel, pages_per_step=g),
        grid_spec=pltpu.PrefetchScalarGridSpec(
            num_scalar_prefetch=1,
            grid=(bsz, n_steps),
            in_specs=in_specs,
            out_specs=row,
            scratch_shapes=[pltpu.VMEM((H_FOX, 1), F32), pltpu.VMEM((H_FOX, 1), F32),
                            pltpu.VMEM((H_FOX, FOX_W), F32), pltpu.VMEM((H_FOX, PAGE_SIZE), F32)]),
        out_shape=jax.ShapeDtypeStruct((bsz, 1, FOX_W), F32),
        compiler_params=pltpu.CompilerParams(
            dimension_semantics=("arbitrary", "arbitrary"), vmem_limit_bytes=VMEM_LIMIT),
        name="fox_sample",
    )(page_table, q3, k3, v3, lf3, w_sfx, *([ck_t] * g), *([cv_t] * g), *([clf_t] * g))
    return out.reshape(bsz, FOX_W)


def _merge_out(x_ref, ro_ref, rg_ref, fo_ref, y_c, seg_ref, retg_ref, wout_ref, out_ref):
    y_ret = jax.nn.silu(rg_ref[...]) * _head_rms(ro_ref[...], seg_ref[...], retg_ref[...])
    y = jnp.concatenate([y_ret, fo_ref[...], y_c], axis=-1).astype(BF16)
    out_ref[...] = x_ref[...] + _dot(y, wout_ref[...])


def _merge_prompt_kernel(x_ref, ro_ref, rg_ref, fo_ref, cu_ref, cv_ref, seg_ref, retg_ref,
                         sw_ref, sb_ref, wout_ref, out_ref):
    tm = x_ref.shape[0]
    r = lax.broadcasted_iota(jnp.int32, (CHUNK, CHUNK), 0)
    c = lax.broadcasted_iota(jnp.int32, (CHUNK, CHUNK), 1)
    lane = lax.broadcasted_iota(jnp.int32, (CHUNK, CMLP_W), 1)
    ws = [jnp.where(r >= c, sw_ref[g], 0.0).astype(BF16) for g in range(C_GROUPS)]
    sb = sb_ref[...]
    chunks = []
    for n in range(tm // CHUNK):
        vb = cv_ref[n * CHUNK:(n + 1) * CHUNK, :].astype(BF16)
        mixed = sb
        for g in range(C_GROUPS):
            mixed = mixed + jnp.where((lane // HEAD_DIM) == g, _dot(ws[g], vb), 0.0)
        chunks.append(cu_ref[n * CHUNK:(n + 1) * CHUNK, :] * mixed)
    y_c = chunks[0] if len(chunks) == 1 else jnp.concatenate(chunks, axis=0)
    _merge_out(x_ref, ro_ref, rg_ref, fo_ref, y_c, seg_ref, retg_ref, wout_ref, out_ref)


def _merge_sample_kernel(x_ref, ro_ref, rg_ref, fo_ref, cu_ref, cv_ref, seg_ref, retg_ref,
                         sw0_ref, sb0_ref, wout_ref, out_ref):
    y_c = cu_ref[...] * (sw0_ref[...] * cv_ref[...] + sb0_ref[...])
    _merge_out(x_ref, ro_ref, rg_ref, fo_ref, y_c, seg_ref, retg_ref, wout_ref, out_ref)


def _merge(x2d, ro, rg, fo, cu, cv, seg, retg, sw, sb, wout, tm, sample):
    n, d = x2d.shape
    row = lambda i: (i, 0)
    const = lambda i: (0, 0)
    if sample:
        kern = _merge_sample_kernel
        sw_spec = pl.BlockSpec((1, CMLP_W), const)
        sb_spec = pl.BlockSpec((1, CMLP_W), const)
    else:
        kern = _merge_prompt_kernel
        sw_spec = pl.BlockSpec((C_GROUPS, CHUNK, CHUNK), lambda i: (0, 0, 0))
        sb_spec = pl.BlockSpec((CHUNK, CMLP_W), const)
    return pl.pallas_call(
        kern,
        grid=(n // tm,),
        in_specs=[pl.BlockSpec((tm, d), row),
                  pl.BlockSpec((tm, RET_W), row), pl.BlockSpec((tm, RET_W), row),
                  pl.BlockSpec((tm, FOX_W), row),
                  pl.BlockSpec((tm, CMLP_W), row), pl.BlockSpec((tm, CMLP_W), row),
                  pl.BlockSpec((256, 256), const), pl.BlockSpec((1, RET_W), const),
                  sw_spec, sb_spec,
                  pl.BlockSpec(wout.shape, const)],
        out_specs=pl.BlockSpec((tm, d), row),
        out_shape=jax.ShapeDtypeStruct((n, d), F32),
        compiler_params=pltpu.CompilerParams(
            dimension_semantics=("arbitrary",), vmem_limit_bytes=VMEM_LIMIT),
        name="merge_sample" if sample else "merge_prompt",
    )(x2d, ro, rg, fo, cu, cv, seg, retg, sw, sb, wout)


def _ffn_prompt_kernel(x_ref, g2_ref, wu_ref, cw_ref, cb_ref, wd_ref, prev_ref, out_ref, st_ref,
                       act_sc, carry_sc, *, tiles_per_seq, fb):
    t = pl.program_id(0) % tiles_per_seq
    tm = x_ref.shape[0]
    d_ff = wd_ref.shape[0]

    @pl.when(t == 0)
    def _():
        carry_sc[...] = prev_ref[...]

    x = x_ref[...]
    hb = _row_rms(x, g2_ref[...]).astype(BF16)
    row = lax.broadcasted_iota(jnp.int32, (tm, fb), 0)

    def conv_cols(lo):
        u = _dot(hb, wu_ref[:, lo:lo + fb])
        prev = carry_sc[:, lo:lo + fb]
        s1 = jnp.where(row == 0, prev[1:2], pltpu.roll(u, 1, 0))
        s2 = jnp.where(row == 0, prev[0:1], jnp.where(row == 1, prev[1:2], pltpu.roll(u, 2, 0)))
        carry_sc[:, lo:lo + fb] = u[tm - (CONV_W - 1):tm, :]
        return (((cb_ref[:, lo:lo + fb] + cw_ref[0:1, lo:lo + fb] * s2)
                 + cw_ref[1:2, lo:lo + fb] * s1) + cw_ref[2:3, lo:lo + fb] * u)

    for c in range(d_ff // fb):
        gate = conv_cols(c * fb)
        val = conv_cols(d_ff + c * fb)
        act_sc[:, c * fb:(c + 1) * fb] = (jax.nn.silu(gate) * val).astype(BF16)
    st_ref[...] = carry_sc[...]
    out_ref[...] = x + _dot(act_sc[...], wd_ref[...])


def _ffn_sample_kernel(x_ref, g2_ref, wug_ref, wuv_ref, cwg_ref, cwv_ref, cbg_ref, cbv_ref, wd_ref,
                       p0g_ref, p0v_ref, p1g_ref, p1v_ref, out_ref, st_ref, h_sc, acc_sc, *, d_ff):
    j = pl.program_id(1)
    fb = wug_ref.shape[1]

    @pl.when(j == 0)
    def _():
        h_sc[...] = _row_rms(x_ref[...], g2_ref[...]).astype(BF16)
        acc_sc[...] = jnp.zeros_like(acc_sc)

    hb = h_sc[...]

    def conv(u, p0, p1, w_ref, b_ref):
        return ((b_ref[...] + w_ref[0:1] * p0) + w_ref[1:2] * p1) + w_ref[2:3] * u

    ug = _dot(hb, wug_ref[...])
    uv = _dot(hb, wuv_ref[...])
    p1g = p1g_ref[...]
    p1v = p1v_ref[...]
    conv_g = conv(ug, p0g_ref[...], p1g, cwg_ref, cbg_ref)
    conv_v = conv(uv, p0v_ref[...], p1v, cwv_ref, cbv_ref)
    cg = pl.multiple_of(j * fb, LANES)
    cv = pl.multiple_of(d_ff + j * fb, LANES)
    st_ref[:, pl.ds(cg, fb)] = p1g
    st_ref[:, pl.ds(cv, fb)] = p1v
    st_ref[:, pl.ds(pl.multiple_of(2 * d_ff + j * fb, LANES), fb)] = ug
    st_ref[:, pl.ds(pl.multiple_of(3 * d_ff + j * fb, LANES), fb)] = uv
    act = (jax.nn.silu(conv_g) * conv_v).astype(BF16)
    acc_sc[...] += _dot(act, wd_ref[...])

    @pl.when(j == pl.num_programs(1) - 1)
    def _():
        out_ref[...] = x_ref[...] + acc_sc[...]


def _ffn(x2d, g2, wup, cw, cb, wdown, prev, tm, fb, seq):
    n, d = x2d.shape
    d_ff = wdown.shape[0]
    nff = d_ff // fb
    nt = n // tm
    xspec = pl.BlockSpec((tm, d), lambda i, j: (i, 0))
    common = [xspec,
              pl.BlockSpec((1, d), lambda i, j: (0, 0)),
              pl.BlockSpec((d, fb), lambda i, j: (0, j)),
              pl.BlockSpec((d, fb), lambda i, j: (0, nff + j)),
              pl.BlockSpec((CONV_W, fb), lambda i, j: (0, j)),
              pl.BlockSpec((CONV_W, fb), lambda i, j: (0, nff + j)),
              pl.BlockSpec((1, fb), lambda i, j: (0, j)),
              pl.BlockSpec((1, fb), lambda i, j: (0, nff + j)),
              pl.BlockSpec((fb, d), lambda i, j: (j, 0))]
    scratch = [pltpu.VMEM((tm, d), BF16), pltpu.VMEM((tm, d), F32)]
    params = pltpu.CompilerParams(
        dimension_semantics=("arbitrary", "arbitrary"), vmem_limit_bytes=VMEM_LIMIT)
    if seq == 1:
        bsz = n
        prev2 = prev.reshape(bsz, (CONV_W - 1) * 2 * d_ff)
        pspecs = [pl.BlockSpec((bsz, fb), lambda i, j, o=o: (0, o + j))
                  for o in (0, nff, 2 * nff, 3 * nff)]
        out, st = pl.pallas_call(
            functools.partial(_ffn_sample_kernel, d_ff=d_ff),
            grid=(nt, nff),
            in_specs=common + [pspecs[0], pspecs[1], pspecs[2], pspecs[3]],
            out_specs=[xspec, pl.BlockSpec(prev2.shape, lambda i, j: (0, 0))],
            out_shape=[jax.ShapeDtypeStruct((n, d), F32), jax.ShapeDtypeStruct(prev2.shape, F32)],
            scratch_shapes=scratch,
            compiler_params=params,
            name="ffn_sample",
        )(x2d, g2, wup, wup, cw, cw, cb, cb, wdown, prev2, prev2, prev2, prev2)
        return out, st.reshape(prev.shape)
    tiles_per_seq = seq // tm
    const = lambda i: (0, 0)
    resident = functools.partial(pl.BlockSpec, pipeline_mode=pl.Buffered(1))
    state = pl.BlockSpec((None, CONV_W - 1, 2 * d_ff), lambda i: (i // tiles_per_seq, 0, 0))
    row = pl.BlockSpec((tm, d), lambda i: (i, 0))
    return pl.pallas_call(
        functools.partial(_ffn_prompt_kernel, tiles_per_seq=tiles_per_seq, fb=fb),
        grid=(nt,),
        in_specs=[row,
                  pl.BlockSpec((1, d), const),
                  resident(wup.shape, const),
                  pl.BlockSpec(cw.shape, const),
                  pl.BlockSpec(cb.shape, const),
                  resident(wdown.shape, const),
                  state],
        out_specs=[row, state],
        out_shape=[jax.ShapeDtypeStruct((n, d), F32), jax.ShapeDtypeStruct(prev.shape, F32)],
        scratch_shapes=[pltpu.VMEM((tm, d_ff), BF16), pltpu.VMEM((CONV_W - 1, 2 * d_ff), F32)],
        compiler_params=pltpu.CompilerParams(
            dimension_semantics=("arbitrary",), vmem_limit_bytes=VMEM_LIMIT),
        name="ffn_prompt",
    )(x2d, g2, wup, cw, cb, wdown, prev)


def _rope_tables(pos):
    half = HEAD_DIM // 2
    inv = 1.0 / (ROPE_BASE ** (jnp.arange(half, dtype=F32) / half))
    ang = pos.astype(F32)[:, None] * inv[None, :]
    cos = jnp.cos(ang)
    sin = jnp.sin(ang)
    cos_e = jnp.tile(jnp.concatenate([cos, cos], axis=-1), (1, H_RET))
    sin_e = jnp.tile(jnp.concatenate([-sin, sin], axis=-1), (1, H_RET))
    return cos_e, sin_e


def _retention_tables():
    c = CHUNK
    lg = jnp.log1p(-jnp.exp2(-5.0 - jnp.arange(H_RET, dtype=F32)))
    idx = jnp.arange(c, dtype=F32)
    diff = idx[:, None] - idx[None, :]
    intra = jnp.where(diff[None] >= 0, jnp.exp(jnp.maximum(diff, 0.0)[None] * lg[:, None, None]), 0.0)
    q_dec = jnp.exp((idx[None, :] + 1.0) * lg[:, None])
    k_dec = jnp.exp((c - 1.0 - idx[None, :]) * lg[:, None])
    chunk_dec = jnp.exp(c * lg)
    expand = lambda a: jnp.repeat(a.T, HEAD_DIM, axis=1)
    head_of = jnp.arange(RET_W) // HEAD_DIM
    bd = (head_of[:, None] == head_of[None, :]).astype(F32)
    decmat = jnp.broadcast_to(chunk_dec[head_of][:, None], (RET_W, RET_W))
    gamma = jnp.exp(1.0 * lg)
    gcol = jnp.broadcast_to(gamma[:, None, None], (H_RET, HEAD_DIM, 1))
    grow = jnp.broadcast_to(gamma[:, None, None], (H_RET, 1, HEAD_DIM))
    return dict(intra=intra, qdec=expand(q_dec), kdec=expand(k_dec), decmat=decmat, bd=bd,
                gcol=gcol, grow=grow)


def _pick_tile(n, target):
    t = min(n, target)
    while n % t:
        t //= 2
    return t


@jax.jit
def kernel(x_prompt, x_sample, cache_k, cache_v, cache_logf, state_ret, state_ffn_conv, page_table,
           norm1_g, w_in, b_forget, ret_norm_g, q_norm_g, k_norm_g, spatial_w, spatial_b, w_out,
           norm2_g, w_up, conv_w, conv_b, w_down):
    bp, lp, d = x_prompt.shape
    bs, ls, _ = x_sample.shape
    depth = w_in.shape[0]
    d_ff = w_down.shape[1]
    n_pages = page_table.shape[1]
    past_len = n_pages * PAGE_SIZE
    assert ls == 1 and lp % CHUNK == 0

    fz0 = 4 * RET_W + 3 * FOX_W
    w_in_p = jnp.concatenate(
        [w_in[..., :fz0], w_in[..., fz0 + H_FOX:], w_in[..., fz0:fz0 + H_FOX],
         jnp.zeros((depth, d, FZ_COLS - H_FOX), w_in.dtype)], axis=-1).astype(BF16)
    w_out_b = w_out.astype(BF16)
    w_up_b = w_up.astype(BF16)
    w_down_b = w_down.astype(BF16)
    bf_p = jnp.pad(b_forget, ((0, 0), (0, FZ_COLS - H_FOX)))
    qg_e = jnp.tile(q_norm_g, (1, H_FOX))
    kg_e = jnp.tile(k_norm_g, (1, H_FOX))
    retg_e = ret_norm_g.reshape(depth, RET_W)
    sb_e = jnp.repeat(jnp.swapaxes(spatial_b, 1, 2), HEAD_DIM, axis=2)
    sw0_e = jnp.repeat(spatial_w[:, :, 0, 0], HEAD_DIM, axis=1)
    sb0_e = jnp.repeat(spatial_b[:, :, 0], HEAD_DIM, axis=1)

    head_of = jnp.arange(256) // HEAD_DIM
    seg = (head_of[:, None] == head_of[None, :]).astype(BF16)
    idx = jnp.arange(CHUNK)
    tri_incl = (idx[:, None] >= idx[None, :]).astype(BF16)
    sfx = (idx[:, None] > idx[None, :]).astype(BF16)
    w_sfx = jnp.concatenate([sfx, jnp.ones((PAGE_SIZE, PAGE_SIZE), BF16)], axis=1)
    ck_t = jnp.transpose(cache_k, (0, 1, 3, 4, 2)).reshape(depth, -1, FOX_W, PAGE_SIZE)
    cv_t = jnp.transpose(cache_v, (0, 1, 3, 4, 2)).reshape(depth, -1, FOX_W, PAGE_SIZE)
    clf_t = jnp.transpose(cache_logf, (0, 1, 3, 2))
    cos_p, sin_p = _rope_tables(jnp.arange(lp))
    cos_s, sin_s = _rope_tables(jnp.broadcast_to(past_len + jnp.arange(ls), (bs,)))
    tabs = _retention_tables()

    tm_p = _pick_tile(lp, 512)
    tq = _pick_tile(lp, 256)
    tk = tq
    tm_f = _pick_tile(lp, 512)
    fb = 256
    g_pages = _pick_tile(n_pages, 16)
    zeros_prev = jnp.zeros((bp, CONV_W - 1, 2 * d_ff), F32)

    xp = x_prompt.reshape(bp * lp, d)
    xs = x_sample.reshape(bs * ls, d)
    outs = [[] for _ in range(11)]
    for l in range(depth):
        g1 = norm1_g[l][None]
        g2 = norm2_g[l][None]
        bfl = bf_p[l][None]
        qg = qg_e[l][None]
        kg = kg_e[l][None]
        retg = retg_e[l][None]
        cb = conv_b[l][None]

        rq, rk, rv, rg, fq, fkt, fvt, _, lfp, cu, cv = _proj(
            xp, cos_p, sin_p, g1, w_in_p[l], bfl, qg, kg, seg, tm_p, True)
        ro, r_state = _ret_prompt(rq, rk, rv, tabs, bp, lp)
        cum, cumt, lft = _cum(lfp, tri_incl, bp, lp)
        fo = _fox_prompt(fq, fkt, fvt, cum, cumt, bp, lp, tq, tk)
        xp = _merge(xp, ro, rg, fo, cu, cv, seg, retg, spatial_w[l], sb_e[l], w_out_b[l], tm_p, False)
        xp, c_state = _ffn(xp, g2, w_up_b[l], conv_w[l], cb, w_down_b[l], zeros_prev, tm_f, fb, lp)
        outs[0].append(fkt)
        outs[1].append(fvt)
        outs[2].append(lft)
        outs[3].append(r_state)
        outs[4].append(c_state)

        rq, rk, rv, rg, fq, fk, fv, lf, lfp, cu, cv = _proj(
            xs, cos_s, sin_s, g1, w_in_p[l], bfl, qg, kg, seg, bs, False)
        ro, r_state = _ret_sample(rq, rk, rv, state_ret, l, tabs)
        fo = _fox_sample(fq, fk, fv, lf, ck_t, cv_t, clf_t, page_table, l, w_sfx, g_pages)
        xs = _merge(xs, ro, rg, fo, cu, cv, seg, retg, sw0_e[l][None], sb0_e[l][None], w_out_b[l],
                    bs, True)
        xs, c_state = _ffn(xs, g2, w_up_b[l], conv_w[l], cb, w_down_b[l], state_ffn_conv[l], bs, fb, 1)
        outs[5].append(fk.reshape(bs, ls, H_FOX, HEAD_DIM))
        outs[6].append(fv.reshape(bs, ls, H_FOX, HEAD_DIM))
        outs[7].append(lf.reshape(bs, ls, H_FOX))
        outs[8].append(r_state)
        outs[9].append(cv.reshape(bs, ls, CMLP_W))
        outs[10].append(c_state)

    stacked = [jnp.stack(o) for o in outs]
    for o in (0, 1):
        stacked[o] = jnp.transpose(stacked[o].reshape(depth, bp, H_FOX, HEAD_DIM, lp), (0, 1, 4, 2, 3))
    stacked[2] = jnp.transpose(stacked[2], (0, 1, 3, 2))
    return (xp.reshape(bp, lp, d), xs.reshape(bs, ls, d), *stacked)
```

```python
import functools

import jax
import jax.numpy as jnp
from jax import lax
from jax.experimental import pallas as pl
from jax.experimental.pallas import tpu as pltpu

HEAD_DIM = 64
H_RET = 4
H_FOX = 8
C_GROUPS = 4
RET_W = H_RET * HEAD_DIM
FOX_W = H_FOX * HEAD_DIM
CMLP_W = C_GROUPS * HEAD_DIM
CHUNK = 128
PAGE_SIZE = 128
CONV_W = 3
RMS_EPS = 1e-6
ROPE_BASE = 10000.0
ATTN_SCALE = HEAD_DIM ** -0.5

LANES = 128
FZ_COLS = LANES
VMEM_LIMIT = 56 * 1024 * 1024
NEG_BIG = -1e30
LOG2E = 1.4426950408889634

F32 = jnp.float32
BF16 = jnp.bfloat16


def _dot(a, b):
    return jnp.dot(a, b, preferred_element_type=F32)


def _dot_nt(a, b):
    return lax.dot_general(a, b, (((1,), (1,)), ((), ())), preferred_element_type=F32)


def _split2(x):
    hi = x.astype(BF16)
    lo = (x - hi.astype(F32)).astype(BF16)
    return hi, lo


def _split3(x):
    a = x.astype(BF16)
    r = x - a.astype(F32)
    b = r.astype(BF16)
    c = (r - b.astype(F32)).astype(BF16)
    return a, b, c


def _seg_mean(sq, seg):
    hi, lo = _split2(sq)
    return (_dot(hi, seg) + _dot(lo, seg)) * (1.0 / HEAD_DIM)


def _head_rms(x, seg, g):
    parts = []
    for c in range(x.shape[-1] // 256):
        xc = x[:, c * 256:(c + 1) * 256]
        parts.append(xc * lax.rsqrt(_seg_mean(xc * xc, seg) + RMS_EPS))
    y = parts[0] if len(parts) == 1 else jnp.concatenate(parts, axis=-1)
    return y * g


def _row_rms(x, g):
    return x * lax.rsqrt(jnp.mean(x * x, axis=-1, keepdims=True) + RMS_EPS) * g


def _proj_kernel(x_ref, cos_ref, sin_ref, g1_ref, w_ref, bf_ref, qg_ref, kg_ref, seg_ref, *refs,
                 kv_transposed):
    (rq_ref, rk_ref, rv_ref, rg_ref, fq_ref, fk_ref, fv_ref, lf_ref, lfp_ref,
     cu_ref, cv_ref) = refs[-11:]
    hb = _row_rms(x_ref[...], g1_ref[...]).astype(BF16)
    seg = seg_ref[...]

    def proj(lo, n):
        return _dot(hb, w_ref[:, lo:lo + n])

    cos = cos_ref[...]
    sin = sin_ref[...]
    lane = lax.broadcasted_iota(jnp.int32, cos.shape, 1)
    first_half = (lane % HEAD_DIM) < (HEAD_DIM // 2)

    def rot(z):
        swapped = jnp.where(first_half, pltpu.roll(z, RET_W - HEAD_DIM // 2, 1),
                            pltpu.roll(z, HEAD_DIM // 2, 1))
        return z * cos + swapped * sin

    rq_ref[...] = rot(proj(0, RET_W))
    rk_ref[...] = rot(proj(RET_W, RET_W)) * ATTN_SCALE
    rv_ref[...] = proj(2 * RET_W, RET_W)
    rg_ref[...] = proj(3 * RET_W, RET_W)
    off = 4 * RET_W
    fq_ref[...] = _head_rms(proj(off, FOX_W), seg, qg_ref[...])
    fk = _head_rms(proj(off + FOX_W, FOX_W), seg, kg_ref[...])
    fv = proj(off + 2 * FOX_W, FOX_W)
    fk_ref[...] = fk.T if kv_transposed else fk
    fv_ref[...] = fv.T if kv_transposed else fv
    off += 3 * FOX_W
    cu_ref[...] = jax.nn.gelu(proj(off, CMLP_W))
    cv_ref[...] = jax.nn.gelu(proj(off + CMLP_W, CMLP_W))
    off += 2 * CMLP_W
    lf = jax.nn.log_sigmoid(proj(off, FZ_COLS) + bf_ref[...])
    lane_z = lax.broadcasted_iota(jnp.int32, lf.shape, 1)
    lfp_ref[...] = jnp.where(lane_z < H_FOX, lf, 0.0)
    lf_ref[...] = lf[:, :H_FOX]


def _proj(x2d, cos, sin, g1, w, bfp, qg, kg, seg, tm, kv_step=None):
    kv_transposed = kv_step is not None
    n, d = x2d.shape
    nt = n // tm
    tt = cos.shape[0] // tm
    row = lambda i: (i, 0)
    const = lambda i: (0, 0)
    outs = [RET_W] * 4 + [FOX_W] * 3 + [H_FOX, FZ_COLS] + [CMLP_W] * 2
    out_specs = [pl.BlockSpec((tm, c), row) for c in outs]
    out_shape = [jax.ShapeDtypeStruct((n, c), F32) for c in outs]
    in_specs = [
        pl.BlockSpec((tm, d), row),
        pl.BlockSpec((tm, RET_W), lambda i: (i % tt, 0)),
        pl.BlockSpec((tm, RET_W), lambda i: (i % tt, 0)),
        pl.BlockSpec((1, d), const),
        pl.BlockSpec(w.shape, const),
        pl.BlockSpec((1, FZ_COLS), const),
        pl.BlockSpec((1, FOX_W), const),
        pl.BlockSpec((1, FOX_W), const),
        pl.BlockSpec((256, 256), const),
    ]
    args = [x2d, cos, sin, g1, w, bfp, qg, kg, seg]
    aliases = {}
    if kv_transposed:
        depth, layer, k_buf, v_buf = kv_step
        for o in (5, 6):
            out_specs[o] = pl.BlockSpec((None, None, FOX_W, tm),
                                        lambda i: (layer, i // tt, 0, i % tt))
            out_shape[o] = jax.ShapeDtypeStruct((depth, nt // tt, FOX_W, tt * tm), F32)
        if k_buf is not None:
            aliases = {len(args): 5, len(args) + 1: 6}
            in_specs += [pl.BlockSpec(memory_space=pl.ANY)] * 2
            args += [k_buf, v_buf]
    return pl.pallas_call(
        functools.partial(_proj_kernel, kv_transposed=kv_transposed),
        grid=(nt,),
        in_specs=in_specs,
        out_specs=out_specs,
        out_shape=out_shape,
        input_output_aliases=aliases,
        compiler_params=pltpu.CompilerParams(
            dimension_semantics=("arbitrary",), vmem_limit_bytes=VMEM_LIMIT),
        name="proj",
    )(*args)


def _ret_prompt_kernel(q_ref, k_ref, v_ref, intra_ref, qdec_ref, kdec_ref, dec_ref, bd_ref,
                       o_ref, st_ref, s_sc, *, chunks):
    c = pl.program_id(1)

    @pl.when(c == 0)
    def _():
        s_sc[...] = jnp.zeros_like(s_sc)

    lane = lax.broadcasted_iota(jnp.int32, (CHUNK, RET_W), 1)
    state = s_sc[...]
    for n in range(chunks):
        sl = slice(n * CHUNK, (n + 1) * CHUNK)
        q = q_ref[sl, :]
        k = k_ref[sl, :]
        vb = v_ref[sl, :].astype(BF16)
        kb = k.astype(BF16)
        o = _dot((q * qdec_ref[...]).astype(BF16), state.astype(BF16))
        for h in range(H_RET):
            hm = (lane // HEAD_DIM) == h
            sc = _dot_nt(jnp.where(hm, q, 0.0).astype(BF16), kb) * intra_ref[h]
            o = o + jnp.where(hm, _dot(sc.astype(BF16), vb), 0.0)
        o_ref[sl, :] = o
        kd = (k * kdec_ref[...]).astype(BF16)
        ktv = lax.dot_general(kd, vb, (((0,), (0,)), ((), ())), preferred_element_type=F32)
        state = state * dec_ref[...] + bd_ref[...] * ktv
    s_sc[...] = state

    @pl.when(c == pl.num_programs(1) - 1)
    def _():
        for h in range(H_RET):
            lo = h * HEAD_DIM
            st_ref[h] = state[lo:lo + HEAD_DIM, lo:lo + HEAD_DIM]


def _ret_prompt(rq, rk, rv, tabs, bsz, seq):
    chunks = _pick_tile(seq // CHUNK, 4)
    rows = chunks * CHUNK
    ns = seq // rows
    blk = pl.BlockSpec((rows, RET_W), lambda b, c: (b * ns + c, 0))
    c2 = lambda b, c: (0, 0)
    return pl.pallas_call(
        functools.partial(_ret_prompt_kernel, chunks=chunks),
        grid=(bsz, ns),
        in_specs=[blk, blk, blk,
                  pl.BlockSpec((H_RET, CHUNK, CHUNK), lambda b, c: (0, 0, 0)),
                  pl.BlockSpec((CHUNK, RET_W), c2),
                  pl.BlockSpec((CHUNK, RET_W), c2),
                  pl.BlockSpec((RET_W, RET_W), c2),
                  pl.BlockSpec((RET_W, RET_W), c2)],
        out_specs=[blk, pl.BlockSpec((None, H_RET, HEAD_DIM, HEAD_DIM), lambda b, c: (b, 0, 0, 0))],
        out_shape=[jax.ShapeDtypeStruct((bsz * seq, RET_W), F32),
                   jax.ShapeDtypeStruct((bsz, H_RET, HEAD_DIM, HEAD_DIM), F32)],
        scratch_shapes=[pltpu.VMEM((RET_W, RET_W), F32)],
        compiler_params=pltpu.CompilerParams(
            dimension_semantics=("arbitrary", "arbitrary"), vmem_limit_bytes=VMEM_LIMIT),
        name="ret_prompt",
    )(rq, rk, rv, tabs["intra"], tabs["qdec"], tabs["kdec"], tabs["decmat"], tabs["bd"])


def _ret_sample_kernel(q_ref, k_ref, v_ref, s_ref, gcol_ref, grow_ref, o_ref, st_ref):
    for h in range(H_RET):
        qc = q_ref[h]
        kc = k_ref[h]
        vr = v_ref[h]
        s = s_ref[h]
        qk = jnp.sum(qc * kc, axis=0, keepdims=True)
        inter = jnp.sum((qc * gcol_ref[h]) * s, axis=0, keepdims=True)
        o_ref[h] = qk * vr + inter
        st_ref[h] = s * grow_ref[h] + kc * vr


def _ret_sample(rq, rk, rv, state, layer, tabs):
    bsz = rq.shape[0]
    qc = rq.reshape(bsz, H_RET, HEAD_DIM, 1)
    kc = rk.reshape(bsz, H_RET, HEAD_DIM, 1)
    vr = rv.reshape(bsz, H_RET, 1, HEAD_DIM)
    col = pl.BlockSpec((None, H_RET, HEAD_DIM, 1), lambda b: (b, 0, 0, 0))
    rowb = pl.BlockSpec((None, H_RET, 1, HEAD_DIM), lambda b: (b, 0, 0, 0))
    stb = pl.BlockSpec((None, H_RET, HEAD_DIM, HEAD_DIM), lambda b: (b, 0, 0, 0))
    o, st = pl.pallas_call(
        _ret_sample_kernel,
        grid=(bsz,),
        in_specs=[col, col, rowb,
                  pl.BlockSpec((None, None, H_RET, HEAD_DIM, HEAD_DIM), lambda b: (layer, b, 0, 0, 0)),
                  pl.BlockSpec((H_RET, HEAD_DIM, 1), lambda b: (0, 0, 0)),
                  pl.BlockSpec((H_RET, 1, HEAD_DIM), lambda b: (0, 0, 0))],
        out_specs=[rowb, stb],
        out_shape=[jax.ShapeDtypeStruct((bsz, H_RET, 1, HEAD_DIM), F32),
                   jax.ShapeDtypeStruct((bsz, H_RET, HEAD_DIM, HEAD_DIM), F32)],
        compiler_params=pltpu.CompilerParams(dimension_semantics=("arbitrary",)),
        name="ret_sample",
    )(qc, kc, vr, state, tabs["gcol"], tabs["grow"])
    return o.reshape(bsz, RET_W), st


def _cum_kernel(lf_ref, tri_ref, cum_ref, cumt_ref, lft_ref):
    tri = tri_ref[...]
    carry = jnp.zeros((1, LANES), F32)
    for c in range(lf_ref.shape[0] // CHUNK):
        sl = slice(c * CHUNK, (c + 1) * CHUNK)
        x = lf_ref[sl, :]
        a, b, r = _split3(x)
        cc = (_dot(tri, a) + _dot(tri, b)) + _dot(tri, r) + carry
        cum_ref[sl, :] = cc
        cumt_ref[:, sl] = cc.T[:H_FOX, :]
        lft_ref[:, sl] = x.T[:H_FOX, :]
        carry = cc[CHUNK - 1:CHUNK, :]


def _cum(lfp, tri, bsz, seq):
    col = pl.BlockSpec((None, H_FOX, seq), lambda b: (b, 0, 0))
    return pl.pallas_call(
        _cum_kernel,
        grid=(bsz,),
        in_specs=[pl.BlockSpec((seq, LANES), lambda b: (b, 0)),
                  pl.BlockSpec((CHUNK, CHUNK), lambda b: (0, 0))],
        out_specs=[pl.BlockSpec((seq, LANES), lambda b: (b, 0)), col, col],
        out_shape=[jax.ShapeDtypeStruct((bsz * seq, LANES), F32),
                   jax.ShapeDtypeStruct((bsz, H_FOX, seq), F32),
                   jax.ShapeDtypeStruct((bsz, H_FOX, seq), F32)],
        compiler_params=pltpu.CompilerParams(dimension_semantics=("arbitrary",)),
        name="cum",
    )(lfp, tri)


def _fox_prompt_kernel(q_ref, kt_ref, vt_ref, cum_ref, cumt_ref, o_ref, k_sc, vt_sc,
                       *, tq, prep_chunk, pairs):
    grp = pl.program_id(1)
    i = pl.program_id(2)
    seq = kt_ref.shape[1]
    nh = 2 * pairs
    head0 = grp * nh

    @pl.when(i == 0)
    def _():
        lane = lax.broadcasted_iota(jnp.int32, (prep_chunk, LANES), 1)
        for c in range(seq // prep_chunk):
            sl = slice(c * prep_chunk, (c + 1) * prep_chunk)
            vt_sc[:, sl] = vt_ref[:, sl].astype(BF16)
            cum = cum_ref[sl, :] * LOG2E
            for pr in range(pairs):
                aux = jnp.where(lane < 3, 1.0, 0.0)
                for hh in range(2):
                    col = jnp.sum(jnp.where(lane == head0 + 2 * pr + hh, cum, 0.0),
                                  axis=-1, keepdims=True)
                    for t, piece in enumerate(_split3(col)):
                        aux = jnp.where(lane == 3 + 3 * hh + t, -piece.astype(F32), aux)
                k_rows = kt_ref[pr * LANES:(pr + 1) * LANES, sl].T
                k_sc[pr, sl, :] = jnp.concatenate([k_rows, aux], axis=1).astype(BF16)

    qt = (q_ref[...] * (ATTN_SCALE * LOG2E)).T
    q0 = pl.multiple_of(i * tq, tq)
    row = lax.broadcasted_iota(jnp.int32, (LANES, tq), 0)
    q_aug = []
    for pr in range(pairs):
        q_pair = qt[pr * LANES:(pr + 1) * LANES]
        for hh in range(2):
            cq = cumt_ref[pl.ds(head0 + 2 * pr + hh, 1), pl.ds(q0, tq)] * LOG2E
            a, b, c = (x.astype(F32) for x in _split3(cq))
            selector = jnp.where((row >= 3 + 3 * hh) & (row < 6 + 3 * hh), 1.0, 0.0)
            bottom = jnp.where(row == 0, a, jnp.where(row == 1, b, jnp.where(row == 2, c, selector)))
            top = jnp.where(row // HEAD_DIM == hh, q_pair, 0.0)
            q_aug.append(jnp.concatenate([top, bottom], axis=0).astype(BF16))

    keep = (lax.broadcasted_iota(jnp.int32, (tq, tq), 0)
            <= lax.broadcasted_iota(jnp.int32, (tq, tq), 1))

    def attend(n_keys):
        below = n_keys - tq
        strips = []
        for pr in range(pairs):
            kb = k_sc[pr, 0:n_keys, :]
            strips += [_dot(kb, q_aug[2 * pr + hh]) for hh in range(2)]
        probs = []
        for s in strips:
            s_diag = jnp.where(keep, s[below:], -jnp.inf)
            m = jnp.max(s_diag, axis=0, keepdims=True)
            if below:
                m = jnp.maximum(m, jnp.max(s[:below], axis=0, keepdims=True))
            p_diag = jnp.exp2(s_diag - m)
            l = jnp.sum(p_diag, axis=0, keepdims=True)
            p = p_diag.astype(BF16)
            if below:
                p_below = jnp.exp2(s[:below] - m)
                l = l + jnp.sum(p_below, axis=0, keepdims=True)
                p = jnp.concatenate([p_below.astype(BF16), p], axis=0)
            probs.append((p, l))
        outs = []
        for h, (p, l) in enumerate(probs):
            vt = vt_sc[h * HEAD_DIM:(h + 1) * HEAD_DIM, 0:n_keys]
            outs.append(_dot(vt, p) / l)
        o_ref[...] = jnp.concatenate(outs, axis=0).T

    for blk in range(seq // tq):
        pl.when(i == blk)(functools.partial(attend, (blk + 1) * tq))


def _fox_prompt(fq, fkt, fvt, layer, cum, cumt, bsz, seq, tq, pairs):
    nq = seq // tq
    width = pairs * LANES
    groups = FOX_W // width
    prep_chunk = _pick_tile(seq, 512)
    kv = pl.BlockSpec((None, None, width, seq), lambda b, g, i: (layer, b, g, 0))
    return pl.pallas_call(
        functools.partial(_fox_prompt_kernel, tq=tq, prep_chunk=prep_chunk, pairs=pairs),
        grid=(bsz, groups, nq),
        in_specs=[pl.BlockSpec((tq, width), lambda b, g, i: (b * nq + i, g)),
                  kv, kv,
                  pl.BlockSpec((seq, LANES), lambda b, g, i: (b, 0)),
                  pl.BlockSpec((None, H_FOX, seq), lambda b, g, i: (b, 0, 0))],
        out_specs=pl.BlockSpec((tq, width), lambda b, g, i: (b * nq + i, g)),
        out_shape=jax.ShapeDtypeStruct((bsz * seq, FOX_W), F32),
        scratch_shapes=[pltpu.VMEM((pairs, seq, 2 * LANES), BF16), pltpu.VMEM((width, seq), BF16)],
        compiler_params=pltpu.CompilerParams(
            dimension_semantics=("arbitrary", "arbitrary", "arbitrary"),
            vmem_limit_bytes=VMEM_LIMIT),
        name="fox_prompt",
    )(fq, fkt, fvt, cum, cumt)


def _fox_sample_kernel(pt_ref, q_ref, kn_ref, vn_ref, lfn_ref, w_ref, *rest, pages_per_step):
    del pt_ref
    g = pages_per_step
    k_refs = rest[:g]
    v_refs = rest[g:2 * g]
    lf_refs = rest[2 * g:3 * g]
    o_ref = rest[3 * g]
    m_sc, l_sc, acc_sc, after_sc = rest[3 * g + 1:]
    jg = pl.program_id(1)

    sub = lax.broadcasted_iota(jnp.int32, (H_FOX, FOX_W), 0)
    lane = lax.broadcasted_iota(jnp.int32, (H_FOX, FOX_W), 1)
    hmask = (lane // HEAD_DIM) == sub
    qrows = jnp.where(hmask, q_ref[...] * ATTN_SCALE, 0.0)

    @pl.when(jg == 0)
    def _():
        m_sc[...] = jnp.full_like(m_sc, NEG_BIG)
        l_sc[...] = jnp.zeros_like(l_sc)
        acc_sc[...] = jnp.zeros_like(acc_sc)
        after_sc[...] = jnp.zeros_like(after_sc)

    qb = qrows.astype(BF16)
    w = w_ref[...]
    m = m_sc[...]
    l = l_sc[...]
    after = after_sc[...]
    lf_all = jnp.concatenate([r[...] for r in lf_refs], axis=0)
    a, b, c = _split3(lf_all)
    st_all = (_dot(a, w) + _dot(b, w)) + _dot(c, w)
    scores = []
    for t in range(g):
        st = st_all[t * H_FOX:(t + 1) * H_FOX]
        kb = k_refs[t][...].astype(BF16)
        scores.append(_dot(qb, kb) + st[:, :PAGE_SIZE] + after)
        after = after + st[:, PAGE_SIZE:]
    s_max = scores[0]
    for t in range(1, g):
        s_max = jnp.maximum(s_max, scores[t])
    m_new = jnp.maximum(m, jnp.max(s_max, axis=-1, keepdims=True))
    alpha = jnp.exp(m - m_new)
    p_sum = None
    pv = None
    for t in range(g):
        p = jnp.exp(scores[t] - m_new)
        p_sum = p if p_sum is None else p_sum + p
        d = _dot_nt(p.astype(BF16), v_refs[t][...].astype(BF16))
        pv = d if pv is None else pv + d
    l = alpha * l + jnp.sum(p_sum, axis=-1, keepdims=True)
    acc = alpha * acc_sc[...] + pv
    m = m_new
    m_sc[...] = m
    l_sc[...] = l
    acc_sc[...] = acc
    after_sc[...] = after

    @pl.when(jg == pl.num_programs(1) - 1)
    def _():
        s_new = jnp.sum(qrows * kn_ref[...], axis=-1, keepdims=True) - lfn_ref[...]
        m_fin = jnp.maximum(m, s_new)
        alpha = jnp.exp(m - m_fin)
        p_new = jnp.exp(s_new - m_fin)
        l_fin = alpha * l + p_new
        acc_fin = alpha * acc + p_new * vn_ref[...]
        out = jnp.where(hmask, acc_fin / l_fin, 0.0)
        o_ref[...] = jnp.sum(out, axis=0, keepdims=True)


def _fox_sample(fq, fk, fv, lf, ck_t, cv_t, clf_t, page_table, layer, w_sfx, g):
    bsz = fq.shape[0]
    n_pages = page_table.shape[1]
    n_steps = n_pages // g
    q3 = fq.reshape(bsz, 1, FOX_W)
    k3 = fk.reshape(bsz, 1, FOX_W)
    v3 = fv.reshape(bsz, 1, FOX_W)
    lf3 = lf.reshape(bsz, H_FOX, 1)
    row = pl.BlockSpec((None, 1, FOX_W), lambda b, j, pt: (b, 0, 0))

    def page_spec(rows, t):
        def imap(b, j, pt):
            return (layer, pt[b, n_pages - 1 - (j * g + t)], 0, 0)
        return pl.BlockSpec((None, None, rows, PAGE_SIZE), imap)

    in_specs = [row, row, row,
                pl.BlockSpec((None, H_FOX, 1), lambda b, j, pt: (b, 0, 0)),
                pl.BlockSpec(w_sfx.shape, lambda b, j, pt: (0, 0))]
    in_specs += [page_spec(FOX_W, t) for t in range(g)]
    in_specs += [page_spec(FOX_W, t) for t in range(g)]
    in_specs += [page_spec(H_FOX, t) for t in range(g)]
    out = pl.pallas_call(
        functools.partial(_fox_sample_kernel, pages_per_step=g),
        grid_spec=pltpu.PrefetchScalarGridSpec(
            num_scalar_prefetch=1,
            grid=(bsz, n_steps),
            in_specs=in_specs,
            out_specs=row,
            scratch_shapes=[pltpu.VMEM((H_FOX, 1), F32), pltpu.VMEM((H_FOX, 1), F32),
                            pltpu.VMEM((H_FOX, FOX_W), F32), pltpu.VMEM((H_FOX, PAGE_SIZE), F32)]),
        out_shape=jax.ShapeDtypeStruct((bsz, 1, FOX_W), F32),
        compiler_params=pltpu.CompilerParams(
            dimension_semantics=("arbitrary", "arbitrary"), vmem_limit_bytes=VMEM_LIMIT),
        name="fox_sample",
    )(page_table, q3, k3, v3, lf3, w_sfx, *([ck_t] * g), *([cv_t] * g), *([clf_t] * g))
    return out.reshape(bsz, FOX_W)


def _merge_out(x_ref, ro_ref, rg_ref, fo_ref, y_c, seg_ref, retg_ref, wout_ref, out_ref):
    y_ret = jax.nn.silu(rg_ref[...]) * _head_rms(ro_ref[...], seg_ref[...], retg_ref[...])
    y = jnp.concatenate([y_ret, fo_ref[...], y_c], axis=-1).astype(BF16)
    out_ref[...] = x_ref[...] + _dot(y, wout_ref[...])


def _merge_prompt_kernel(x_ref, ro_ref, rg_ref, fo_ref, cu_ref, cv_ref, seg_ref, retg_ref,
                         sw_ref, sb_ref, wout_ref, out_ref):
    tm = x_ref.shape[0]
    r = lax.broadcasted_iota(jnp.int32, (CHUNK, CHUNK), 0)
    c = lax.broadcasted_iota(jnp.int32, (CHUNK, CHUNK), 1)
    lane = lax.broadcasted_iota(jnp.int32, (CHUNK, CMLP_W), 1)
    ws = [jnp.where(r >= c, sw_ref[g], 0.0).astype(BF16) for g in range(C_GROUPS)]
    sb = sb_ref[...]
    chunks = []
    for n in range(tm // CHUNK):
        vb = cv_ref[n * CHUNK:(n + 1) * CHUNK, :].astype(BF16)
        mixed = sb
        for g in range(C_GROUPS):
            mixed = mixed + jnp.where((lane // HEAD_DIM) == g, _dot(ws[g], vb), 0.0)
        chunks.append(cu_ref[n * CHUNK:(n + 1) * CHUNK, :] * mixed)
    y_c = chunks[0] if len(chunks) == 1 else jnp.concatenate(chunks, axis=0)
    _merge_out(x_ref, ro_ref, rg_ref, fo_ref, y_c, seg_ref, retg_ref, wout_ref, out_ref)


def _merge_sample_kernel(x_ref, ro_ref, rg_ref, fo_ref, cu_ref, cv_ref, seg_ref, retg_ref,
                         sw0_ref, sb0_ref, wout_ref, out_ref):
    y_c = cu_ref[...] * (sw0_ref[...] * cv_ref[...] + sb0_ref[...])
    _merge_out(x_ref, ro_ref, rg_ref, fo_ref, y_c, seg_ref, retg_ref, wout_ref, out_ref)


def _merge(x2d, ro, rg, fo, cu, cv, seg, retg, sw, sb, wout, tm, sample):
    n, d = x2d.shape
    row = lambda i: (i, 0)
    const = lambda i: (0, 0)
    if sample:
        kern = _merge_sample_kernel
        sw_spec = pl.BlockSpec((1, CMLP_W), const)
        sb_spec = pl.BlockSpec((1, CMLP_W), const)
    else:
        kern = _merge_prompt_kernel
        sw_spec = pl.BlockSpec((C_GROUPS, CHUNK, CHUNK), lambda i: (0, 0, 0))
        sb_spec = pl.BlockSpec((CHUNK, CMLP_W), const)
    return pl.pallas_call(
        kern,
        grid=(n // tm,),
        in_specs=[pl.BlockSpec((tm, d), row),
                  pl.BlockSpec((tm, RET_W), row), pl.BlockSpec((tm, RET_W), row),
                  pl.BlockSpec((tm, FOX_W), row),
                  pl.BlockSpec((tm, CMLP_W), row), pl.BlockSpec((tm, CMLP_W), row),
                  pl.BlockSpec((256, 256), const), pl.BlockSpec((1, RET_W), const),
                  sw_spec, sb_spec,
                  pl.BlockSpec(wout.shape, const)],
        out_specs=pl.BlockSpec((tm, d), row),
        out_shape=jax.ShapeDtypeStruct((n, d), F32),
        compiler_params=pltpu.CompilerParams(
            dimension_semantics=("arbitrary",), vmem_limit_bytes=VMEM_LIMIT),
        name="merge_sample" if sample else "merge_prompt",
    )(x2d, ro, rg, fo, cu, cv, seg, retg, sw, sb, wout)


def _ffn_prompt_kernel(x_ref, g2_ref, wu_ref, cw_ref, cb_ref, wd_ref, prev_ref, out_ref, st_ref,
                       act_sc, carry_sc, *, tiles_per_seq, fb):
    t = pl.program_id(0) % tiles_per_seq
    tm = x_ref.shape[0]
    d_ff = wd_ref.shape[0]

    @pl.when(t == 0)
    def _():
        carry_sc[...] = prev_ref[...]

    x = x_ref[...]
    hb = _row_rms(x, g2_ref[...]).astype(BF16)
    row = lax.broadcasted_iota(jnp.int32, (tm, fb), 0)

    def conv_cols(lo):
        u = _dot(hb, wu_ref[:, lo:lo + fb])
        prev = carry_sc[:, lo:lo + fb]
        s1 = jnp.where(row == 0, prev[1:2], pltpu.roll(u, 1, 0))
        s2 = jnp.where(row == 0, prev[0:1], jnp.where(row == 1, prev[1:2], pltpu.roll(u, 2, 0)))
        carry_sc[:, lo:lo + fb] = u[tm - (CONV_W - 1):tm, :]
        return (((cb_ref[:, lo:lo + fb] + cw_ref[0:1, lo:lo + fb] * s2)
                 + cw_ref[1:2, lo:lo + fb] * s1) + cw_ref[2:3, lo:lo + fb] * u)

    for c in range(d_ff // fb):
        gate = conv_cols(c * fb)
        val = conv_cols(d_ff + c * fb)
        act_sc[:, c * fb:(c + 1) * fb] = (jax.nn.silu(gate) * val).astype(BF16)
    st_ref[...] = carry_sc[...]
    out_ref[...] = x + _dot(act_sc[...], wd_ref[...])


def _ffn_sample_kernel(x_ref, g2_ref, wug_ref, wuv_ref, cwg_ref, cwv_ref, cbg_ref, cbv_ref, wd_ref,
                       p0g_ref, p0v_ref, p1g_ref, p1v_ref, out_ref, st_ref, h_sc, acc_sc, *, d_ff):
    j = pl.program_id(1)
    fb = wug_ref.shape[1]

    @pl.when(j == 0)
    def _():
        h_sc[...] = _row_rms(x_ref[...], g2_ref[...]).astype(BF16)
        acc_sc[...] = jnp.zeros_like(acc_sc)

    hb = h_sc[...]

    def conv(u, p0, p1, w_ref, b_ref):
        return ((b_ref[...] + w_ref[0:1] * p0) + w_ref[1:2] * p1) + w_ref[2:3] * u

    ug = _dot(hb, wug_ref[...])
    uv = _dot(hb, wuv_ref[...])
    p1g = p1g_ref[...]
    p1v = p1v_ref[...]
    conv_g = conv(ug, p0g_ref[...], p1g, cwg_ref, cbg_ref)
    conv_v = conv(uv, p0v_ref[...], p1v, cwv_ref, cbv_ref)
    cg = pl.multiple_of(j * fb, LANES)
    cv = pl.multiple_of(d_ff + j * fb, LANES)
    st_ref[:, pl.ds(cg, fb)] = p1g
    st_ref[:, pl.ds(cv, fb)] = p1v
    st_ref[:, pl.ds(pl.multiple_of(2 * d_ff + j * fb, LANES), fb)] = ug
    st_ref[:, pl.ds(pl.multiple_of(3 * d_ff + j * fb, LANES), fb)] = uv
    act = (jax.nn.silu(conv_g) * conv_v).astype(BF16)
    acc_sc[...] += _dot(act, wd_ref[...])

    @pl.when(j == pl.num_programs(1) - 1)
    def _():
        out_ref[...] = x_ref[...] + acc_sc[...]


def _ffn(x2d, g2, wup, cw, cb, wdown, prev, tm, fb, seq):
    n, d = x2d.shape
    d_ff = wdown.shape[0]
    nff = d_ff // fb
    nt = n // tm
    xspec = pl.BlockSpec((tm, d), lambda i, j: (i, 0))
    common = [xspec,
              pl.BlockSpec((1, d), lambda i, j: (0, 0)),
              pl.BlockSpec((d, fb), lambda i, j: (0, j)),
              pl.BlockSpec((d, fb), lambda i, j: (0, nff + j)),
              pl.BlockSpec((CONV_W, fb), lambda i, j: (0, j)),
              pl.BlockSpec((CONV_W, fb), lambda i, j: (0, nff + j)),
              pl.BlockSpec((1, fb), lambda i, j: (0, j)),
              pl.BlockSpec((1, fb), lambda i, j: (0, nff + j)),
              pl.BlockSpec((fb, d), lambda i, j: (j, 0))]
    scratch = [pltpu.VMEM((tm, d), BF16), pltpu.VMEM((tm, d), F32)]
    params = pltpu.CompilerParams(
        dimension_semantics=("arbitrary", "arbitrary"), vmem_limit_bytes=VMEM_LIMIT)
    if seq == 1:
        bsz = n
        prev2 = prev.reshape(bsz, (CONV_W - 1) * 2 * d_ff)
        pspecs = [pl.BlockSpec((bsz, fb), lambda i, j, o=o: (0, o + j))
                  for o in (0, nff, 2 * nff, 3 * nff)]
        out, st = pl.pallas_call(
            functools.partial(_ffn_sample_kernel, d_ff=d_ff),
            grid=(nt, nff),
            in_specs=common + [pspecs[0], pspecs[1], pspecs[2], pspecs[3]],
            out_specs=[xspec, pl.BlockSpec(prev2.shape, lambda i, j: (0, 0))],
            out_shape=[jax.ShapeDtypeStruct((n, d), F32), jax.ShapeDtypeStruct(prev2.shape, F32)],
            scratch_shapes=scratch,
            compiler_params=params,
            name="ffn_sample",
        )(x2d, g2, wup, wup, cw, cw, cb, cb, wdown, prev2, prev2, prev2, prev2)
        return out, st.reshape(prev.shape)
    tiles_per_seq = seq // tm
    const = lambda i: (0, 0)
    resident = functools.partial(pl.BlockSpec, pipeline_mode=pl.Buffered(1))
    state = pl.BlockSpec((None, CONV_W - 1, 2 * d_ff), lambda i: (i // tiles_per_seq, 0, 0))
    row = pl.BlockSpec((tm, d), lambda i: (i, 0))
    return pl.pallas_call(
        functools.partial(_ffn_prompt_kernel, tiles_per_seq=tiles_per_seq, fb=fb),
        grid=(nt,),
        in_specs=[row,
                  pl.BlockSpec((1, d), const),
                  resident(wup.shape, const),
                  pl.BlockSpec(cw.shape, const),
                  pl.BlockSpec(cb.shape, const),
                  resident(wdown.shape, const),
                  state],
        out_specs=[row, state],
        out_shape=[jax.ShapeDtypeStruct((n, d), F32), jax.ShapeDtypeStruct(prev.shape, F32)],
        scratch_shapes=[pltpu.VMEM((tm, d_ff), BF16), pltpu.VMEM((CONV_W - 1, 2 * d_ff), F32)],
        compiler_params=pltpu.CompilerParams(
            dimension_semantics=("arbitrary",), vmem_limit_bytes=VMEM_LIMIT),
        name="ffn_prompt",
    )(x2d, g2, wup, cw, cb, wdown, prev)


def _rope_tables(pos):
    half = HEAD_DIM // 2
    inv = 1.0 / (ROPE_BASE ** (jnp.arange(half, dtype=F32) / half))
    ang = pos.astype(F32)[:, None] * inv[None, :]
    cos = jnp.cos(ang)
    sin = jnp.sin(ang)
    cos_e = jnp.tile(jnp.concatenate([cos, cos], axis=-1), (1, H_RET))
    sin_e = jnp.tile(jnp.concatenate([-sin, sin], axis=-1), (1, H_RET))
    return cos_e, sin_e


def _retention_tables():
    c = CHUNK
    lg = jnp.log1p(-jnp.exp2(-5.0 - jnp.arange(H_RET, dtype=F32)))
    idx = jnp.arange(c, dtype=F32)
    diff = idx[:, None] - idx[None, :]
    intra = jnp.where(diff[None] >= 0, jnp.exp(jnp.maximum(diff, 0.0)[None] * lg[:, None, None]), 0.0)
    q_dec = jnp.exp((idx[None, :] + 1.0) * lg[:, None])
    k_dec = jnp.exp((c - 1.0 - idx[None, :]) * lg[:, None])
    chunk_dec = jnp.exp(c * lg)
    expand = lambda a: jnp.repeat(a.T, HEAD_DIM, axis=1)
    head_of = jnp.arange(RET_W) // HEAD_DIM
    bd = (head_of[:, None] == head_of[None, :]).astype(F32)
    decmat = jnp.broadcast_to(chunk_dec[head_of][:, None], (RET_W, RET_W))
    gamma = jnp.exp(1.0 * lg)
    gcol = jnp.broadcast_to(gamma[:, None, None], (H_RET, HEAD_DIM, 1))
    grow = jnp.broadcast_to(gamma[:, None, None], (H_RET, 1, HEAD_DIM))
    return dict(intra=intra, qdec=expand(q_dec), kdec=expand(k_dec), decmat=decmat, bd=bd,
                gcol=gcol, grow=grow)


def _pick_tile(n, target):
    t = min(n, target)
    while n % t:
        t //= 2
    return t


@jax.jit
def kernel(x_prompt, x_sample, cache_k, cache_v, cache_logf, state_ret, state_ffn_conv, page_table,
           norm1_g, w_in, b_forget, ret_norm_g, q_norm_g, k_norm_g, spatial_w, spatial_b, w_out,
           norm2_g, w_up, conv_w, conv_b, w_down):
    bp, lp, d = x_prompt.shape
    bs, ls, _ = x_sample.shape
    depth = w_in.shape[0]
    d_ff = w_down.shape[1]
    n_pages = page_table.shape[1]
    past_len = n_pages * PAGE_SIZE
    assert ls == 1 and lp % CHUNK == 0

    fz0 = 4 * RET_W + 3 * FOX_W
    w_in_p = jnp.concatenate(
        [w_in[..., :fz0], w_in[..., fz0 + H_FOX:], w_in[..., fz0:fz0 + H_FOX],
         jnp.zeros((depth, d, FZ_COLS - H_FOX), w_in.dtype)], axis=-1).astype(BF16)
    w_out_b = w_out.astype(BF16)
    w_up_b = w_up.astype(BF16)
    w_down_b = w_down.astype(BF16)
    bf_p = jnp.pad(b_forget, ((0, 0), (0, FZ_COLS - H_FOX)))
    qg_e = jnp.tile(q_norm_g, (1, H_FOX))
    kg_e = jnp.tile(k_norm_g, (1, H_FOX))
    retg_e = ret_norm_g.reshape(depth, RET_W)
    sb_e = jnp.repeat(jnp.swapaxes(spatial_b, 1, 2), HEAD_DIM, axis=2)
    sw0_e = jnp.repeat(spatial_w[:, :, 0, 0], HEAD_DIM, axis=1)
    sb0_e = jnp.repeat(spatial_b[:, :, 0], HEAD_DIM, axis=1)

    head_of = jnp.arange(256) // HEAD_DIM
    seg = (head_of[:, None] == head_of[None, :]).astype(BF16)
    idx = jnp.arange(CHUNK)
    tri_incl = (idx[:, None] >= idx[None, :]).astype(BF16)
    sfx = (idx[:, None] > idx[None, :]).astype(BF16)
    w_sfx = jnp.concatenate([sfx, jnp.ones((PAGE_SIZE, PAGE_SIZE), BF16)], axis=1)
    ck_t = jnp.transpose(cache_k, (0, 1, 3, 4, 2)).reshape(depth, -1, FOX_W, PAGE_SIZE)
    cv_t = jnp.transpose(cache_v, (0, 1, 3, 4, 2)).reshape(depth, -1, FOX_W, PAGE_SIZE)
    clf_t = jnp.transpose(cache_logf, (0, 1, 3, 2))
    cos_p, sin_p = _rope_tables(jnp.arange(lp))
    cos_s, sin_s = _rope_tables(jnp.broadcast_to(past_len + jnp.arange(ls), (bs,)))
    tabs = _retention_tables()

    tm_p = _pick_tile(lp, 512)
    tq = _pick_tile(lp, 256)
    tm_f = _pick_tile(lp, 512)
    fb = 256
    g_pages = _pick_tile(n_pages, 16)
    zeros_prev = jnp.zeros((bp, CONV_W - 1, 2 * d_ff), F32)

    xp = x_prompt.reshape(bp * lp, d)
    xs = x_sample.reshape(bs * ls, d)
    outs = [[] for _ in range(11)]
    fkt = fvt = None
    for l in range(depth):
        g1 = norm1_g[l][None]
        g2 = norm2_g[l][None]
        bfl = bf_p[l][None]
        qg = qg_e[l][None]
        kg = kg_e[l][None]
        retg = retg_e[l][None]
        cb = conv_b[l][None]

        rq, rk, rv, rg, fq, fkt, fvt, _, lfp, cu, cv = _proj(
            xp, cos_p, sin_p, g1, w_in_p[l], bfl, qg, kg, seg, tm_p, (depth, l, fkt, fvt))
        ro, r_state = _ret_prompt(rq, rk, rv, tabs, bp, lp)
        cum, cumt, lft = _cum(lfp, tri_incl, bp, lp)
        fo = _fox_prompt(fq, fkt, fvt, l, cum, cumt, bp, lp, tq, 2)
        xp = _merge(xp, ro, rg, fo, cu, cv, seg, retg, spatial_w[l], sb_e[l], w_out_b[l], tm_p, False)
        xp, c_state = _ffn(xp, g2, w_up_b[l], conv_w[l], cb, w_down_b[l], zeros_prev, tm_f, fb, lp)
        outs[2].append(lft)
        outs[3].append(r_state)
        outs[4].append(c_state)

        rq, rk, rv, rg, fq, fk, fv, lf, lfp, cu, cv = _proj(
            xs, cos_s, sin_s, g1, w_in_p[l], bfl, qg, kg, seg, bs)
        ro, r_state = _ret_sample(rq, rk, rv, state_ret, l, tabs)
        fo = _fox_sample(fq, fk, fv, lf, ck_t, cv_t, clf_t, page_table, l, w_sfx, g_pages)
        xs = _merge(xs, ro, rg, fo, cu, cv, seg, retg, sw0_e[l][None], sb0_e[l][None], w_out_b[l],
                    bs, True)
        xs, c_state = _ffn(xs, g2, w_up_b[l], conv_w[l], cb, w_down_b[l], state_ffn_conv[l], bs, fb, 1)
        outs[5].append(fk.reshape(bs, ls, H_FOX, HEAD_DIM))
        outs[6].append(fv.reshape(bs, ls, H_FOX, HEAD_DIM))
        outs[7].append(lf.reshape(bs, ls, H_FOX))
        outs[8].append(r_state)
        outs[9].append(cv.reshape(bs, ls, CMLP_W))
        outs[10].append(c_state)

    stacked = [jnp.stack(o) if o else None for o in outs]
    for o, buf in ((0, fkt), (1, fvt)):
        stacked[o] = jnp.transpose(buf.reshape(depth, bp, H_FOX, HEAD_DIM, lp), (0, 1, 4, 2, 3))
    stacked[2] = jnp.transpose(stacked[2], (0, 1, 3, 2))
    return (xp.reshape(bp, lp, d), xs.reshape(bs, ls, d), *stacked)
```

```python
import functools

import jax
import jax.numpy as jnp
from jax import lax
from jax.experimental import pallas as pl
from jax.experimental.pallas import tpu as pltpu

HEAD_DIM = 64
H_RET = 4
H_FOX = 8
C_GROUPS = 4
RET_W = H_RET * HEAD_DIM
FOX_W = H_FOX * HEAD_DIM
CMLP_W = C_GROUPS * HEAD_DIM
CHUNK = 128
PAGE_SIZE = 128
CONV_W = 3
RMS_EPS = 1e-6
ROPE_BASE = 10000.0
ATTN_SCALE = HEAD_DIM ** -0.5

LANES = 128
FZ_COLS = LANES
VMEM_LIMIT = 56 * 1024 * 1024
NEG_BIG = -1e30
LOG2E = 1.4426950408889634

F32 = jnp.float32
BF16 = jnp.bfloat16


def _dot(a, b):
    return jnp.dot(a, b, preferred_element_type=F32)


def _dot_nt(a, b):
    return lax.dot_general(a, b, (((1,), (1,)), ((), ())), preferred_element_type=F32)


def _split2(x):
    hi = x.astype(BF16)
    lo = (x - hi.astype(F32)).astype(BF16)
    return hi, lo


def _split3(x):
    a = x.astype(BF16)
    r = x - a.astype(F32)
    b = r.astype(BF16)
    c = (r - b.astype(F32)).astype(BF16)
    return a, b, c


def _seg_mean(sq, seg):
    hi, lo = _split2(sq)
    return (_dot(hi, seg) + _dot(lo, seg)) * (1.0 / HEAD_DIM)


def _head_rms(x, seg, g):
    parts = []
    for c in range(x.shape[-1] // 256):
        xc = x[:, c * 256:(c + 1) * 256]
        parts.append(xc * lax.rsqrt(_seg_mean(xc * xc, seg) + RMS_EPS))
    y = parts[0] if len(parts) == 1 else jnp.concatenate(parts, axis=-1)
    return y * g


def _row_rms(x, g):
    return x * lax.rsqrt(jnp.mean(x * x, axis=-1, keepdims=True) + RMS_EPS) * g


def _proj_kernel(x_ref, cos_ref, sin_ref, g1_ref, w_ref, bf_ref, qg_ref, kg_ref, seg_ref, *refs,
                 kv_transposed):
    (rq_ref, rk_ref, rv_ref, rg_ref, fq_ref, fk_ref, fv_ref, lf_ref, lfp_ref,
     cu_ref, cv_ref) = refs[-11:]
    hb = _row_rms(x_ref[...], g1_ref[...]).astype(BF16)
    seg = seg_ref[...]

    def proj(lo, n):
        return _dot(hb, w_ref[:, lo:lo + n])

    cos = cos_ref[...]
    sin = sin_ref[...]
    lane = lax.broadcasted_iota(jnp.int32, cos.shape, 1)
    first_half = (lane % HEAD_DIM) < (HEAD_DIM // 2)

    def rot(z):
        swapped = jnp.where(first_half, pltpu.roll(z, RET_W - HEAD_DIM // 2, 1),
                            pltpu.roll(z, HEAD_DIM // 2, 1))
        return z * cos + swapped * sin

    rq_ref[...] = rot(proj(0, RET_W))
    rk_ref[...] = rot(proj(RET_W, RET_W)) * ATTN_SCALE
    rv_ref[...] = proj(2 * RET_W, RET_W)
    rg_ref[...] = proj(3 * RET_W, RET_W)
    off = 4 * RET_W
    fq_ref[...] = _head_rms(proj(off, FOX_W), seg, qg_ref[...])
    fk = _head_rms(proj(off + FOX_W, FOX_W), seg, kg_ref[...])
    fv = proj(off + 2 * FOX_W, FOX_W)
    fk_ref[...] = fk.T if kv_transposed else fk
    fv_ref[...] = fv.T if kv_transposed else fv
    off += 3 * FOX_W
    cu_ref[...] = jax.nn.gelu(proj(off, CMLP_W))
    cv_ref[...] = jax.nn.gelu(proj(off + CMLP_W, CMLP_W))
    off += 2 * CMLP_W
    lf = jax.nn.log_sigmoid(proj(off, FZ_COLS) + bf_ref[...])
    lane_z = lax.broadcasted_iota(jnp.int32, lf.shape, 1)
    lfp_ref[...] = jnp.where(lane_z < H_FOX, lf, 0.0)
    lf_ref[...] = lf[:, :H_FOX]


def _proj(x2d, cos, sin, g1, w, bfp, qg, kg, seg, tm, kv_step=None):
    kv_transposed = kv_step is not None
    n, d = x2d.shape
    nt = n // tm
    tt = cos.shape[0] // tm
    row = lambda i: (i, 0)
    const = lambda i: (0, 0)
    outs = [RET_W] * 4 + [FOX_W] * 3 + [H_FOX, FZ_COLS] + [CMLP_W] * 2
    out_specs = [pl.BlockSpec((tm, c), row) for c in outs]
    out_shape = [jax.ShapeDtypeStruct((n, c), F32) for c in outs]
    in_specs = [
        pl.BlockSpec((tm, d), row),
        pl.BlockSpec((tm, RET_W), lambda i: (i % tt, 0)),
        pl.BlockSpec((tm, RET_W), lambda i: (i % tt, 0)),
        pl.BlockSpec((1, d), const),
        pl.BlockSpec(w.shape, const),
        pl.BlockSpec((1, FZ_COLS), const),
        pl.BlockSpec((1, FOX_W), const),
        pl.BlockSpec((1, FOX_W), const),
        pl.BlockSpec((256, 256), const),
    ]
    args = [x2d, cos, sin, g1, w, bfp, qg, kg, seg]
    aliases = {}
    if kv_transposed:
        depth, layer, k_buf, v_buf = kv_step
        for o in (5, 6):
            out_specs[o] = pl.BlockSpec((None, None, FOX_W, tm),
                                        lambda i: (layer, i // tt, 0, i % tt))
            out_shape[o] = jax.ShapeDtypeStruct((depth, nt // tt, FOX_W, tt * tm), F32)
        if k_buf is not None:
            aliases = {len(args): 5, len(args) + 1: 6}
            in_specs += [pl.BlockSpec(memory_space=pl.ANY)] * 2
            args += [k_buf, v_buf]
    return pl.pallas_call(
        functools.partial(_proj_kernel, kv_transposed=kv_transposed),
        grid=(nt,),
        in_specs=in_specs,
        out_specs=out_specs,
        out_shape=out_shape,
        input_output_aliases=aliases,
        compiler_params=pltpu.CompilerParams(
            dimension_semantics=("arbitrary",), vmem_limit_bytes=VMEM_LIMIT),
        name="proj",
    )(*args)


def _ret_prompt_kernel(q_ref, k_ref, v_ref, intra_ref, qdec_ref, kdec_ref, dec_ref, bd_ref,
                       o_ref, st_ref, s_sc, *, chunks):
    c = pl.program_id(1)

    @pl.when(c == 0)
    def _():
        s_sc[...] = jnp.zeros_like(s_sc)

    lane = lax.broadcasted_iota(jnp.int32, (CHUNK, RET_W), 1)
    state = s_sc[...]
    for n in range(chunks):
        sl = slice(n * CHUNK, (n + 1) * CHUNK)
        q = q_ref[sl, :]
        k = k_ref[sl, :]
        vb = v_ref[sl, :].astype(BF16)
        kb = k.astype(BF16)
        o = _dot((q * qdec_ref[...]).astype(BF16), state.astype(BF16))
        for h in range(H_RET):
            hm = (lane // HEAD_DIM) == h
            sc = _dot_nt(jnp.where(hm, q, 0.0).astype(BF16), kb) * intra_ref[h]
            o = o + jnp.where(hm, _dot(sc.astype(BF16), vb), 0.0)
        o_ref[sl, :] = o
        kd = (k * kdec_ref[...]).astype(BF16)
        ktv = lax.dot_general(kd, vb, (((0,), (0,)), ((), ())), preferred_element_type=F32)
        state = state * dec_ref[...] + bd_ref[...] * ktv
    s_sc[...] = state

    @pl.when(c == pl.num_programs(1) - 1)
    def _():
        for h in range(H_RET):
            lo = h * HEAD_DIM
            st_ref[h] = state[lo:lo + HEAD_DIM, lo:lo + HEAD_DIM]


def _ret_prompt(rq, rk, rv, tabs, bsz, seq):
    chunks = _pick_tile(seq // CHUNK, 4)
    rows = chunks * CHUNK
    ns = seq // rows
    blk = pl.BlockSpec((rows, RET_W), lambda b, c: (b * ns + c, 0))
    c2 = lambda b, c: (0, 0)
    return pl.pallas_call(
        functools.partial(_ret_prompt_kernel, chunks=chunks),
        grid=(bsz, ns),
        in_specs=[blk, blk, blk,
                  pl.BlockSpec((H_RET, CHUNK, CHUNK), lambda b, c: (0, 0, 0)),
                  pl.BlockSpec((CHUNK, RET_W), c2),
                  pl.BlockSpec((CHUNK, RET_W), c2),
                  pl.BlockSpec((RET_W, RET_W), c2),
                  pl.BlockSpec((RET_W, RET_W), c2)],
        out_specs=[blk, pl.BlockSpec((None, H_RET, HEAD_DIM, HEAD_DIM), lambda b, c: (b, 0, 0, 0))],
        out_shape=[jax.ShapeDtypeStruct((bsz * seq, RET_W), F32),
                   jax.ShapeDtypeStruct((bsz, H_RET, HEAD_DIM, HEAD_DIM), F32)],
        scratch_shapes=[pltpu.VMEM((RET_W, RET_W), F32)],
        compiler_params=pltpu.CompilerParams(
            dimension_semantics=("arbitrary", "arbitrary"), vmem_limit_bytes=VMEM_LIMIT),
        name="ret_prompt",
    )(rq, rk, rv, tabs["intra"], tabs["qdec"], tabs["kdec"], tabs["decmat"], tabs["bd"])


def _ret_sample_kernel(q_ref, k_ref, v_ref, s_ref, gcol_ref, grow_ref, o_ref, st_ref):
    for b in range(q_ref.shape[0]):
        for h in range(H_RET):
            qc = q_ref[b, h]
            kc = k_ref[b, h]
            vr = v_ref[b, h]
            s = s_ref[b, h]
            qk = jnp.sum(qc * kc, axis=0, keepdims=True)
            inter = jnp.sum((qc * gcol_ref[h]) * s, axis=0, keepdims=True)
            o_ref[b, h] = qk * vr + inter
            st_ref[b, h] = s * grow_ref[h] + kc * vr


def _ret_sample(rq, rk, rv, state, layer, tabs):
    bsz = rq.shape[0]
    nb = _pick_tile(bsz, 8)
    qc = rq.reshape(bsz, H_RET, HEAD_DIM, 1)
    kc = rk.reshape(bsz, H_RET, HEAD_DIM, 1)
    vr = rv.reshape(bsz, H_RET, 1, HEAD_DIM)
    col = pl.BlockSpec((nb, H_RET, HEAD_DIM, 1), lambda b: (b, 0, 0, 0))
    rowb = pl.BlockSpec((nb, H_RET, 1, HEAD_DIM), lambda b: (b, 0, 0, 0))
    stb = pl.BlockSpec((nb, H_RET, HEAD_DIM, HEAD_DIM), lambda b: (b, 0, 0, 0))
    o, st = pl.pallas_call(
        _ret_sample_kernel,
        grid=(bsz // nb,),
        in_specs=[col, col, rowb,
                  pl.BlockSpec((None, nb, H_RET, HEAD_DIM, HEAD_DIM), lambda b: (layer, b, 0, 0, 0)),
                  pl.BlockSpec((H_RET, HEAD_DIM, 1), lambda b: (0, 0, 0)),
                  pl.BlockSpec((H_RET, 1, HEAD_DIM), lambda b: (0, 0, 0))],
        out_specs=[rowb, stb],
        out_shape=[jax.ShapeDtypeStruct((bsz, H_RET, 1, HEAD_DIM), F32),
                   jax.ShapeDtypeStruct((bsz, H_RET, HEAD_DIM, HEAD_DIM), F32)],
        compiler_params=pltpu.CompilerParams(dimension_semantics=("arbitrary",)),
        name="ret_sample",
    )(qc, kc, vr, state, tabs["gcol"], tabs["grow"])
    return o.reshape(bsz, RET_W), st


def _cum_kernel(lf_ref, tri_ref, cum_ref, cumt_ref, lft_ref):
    tri = tri_ref[...]
    carry = jnp.zeros((1, LANES), F32)
    for c in range(lf_ref.shape[0] // CHUNK):
        sl = slice(c * CHUNK, (c + 1) * CHUNK)
        x = lf_ref[sl, :]
        a, b, r = _split3(x)
        cc = (_dot(tri, a) + _dot(tri, b)) + _dot(tri, r) + carry
        cum_ref[sl, :] = cc
        cumt_ref[:, sl] = cc.T[:H_FOX, :]
        lft_ref[:, sl] = x.T[:H_FOX, :]
        carry = cc[CHUNK - 1:CHUNK, :]


def _cum(lfp, tri, bsz, seq):
    col = pl.BlockSpec((None, H_FOX, seq), lambda b: (b, 0, 0))
    return pl.pallas_call(
        _cum_kernel,
        grid=(bsz,),
        in_specs=[pl.BlockSpec((seq, LANES), lambda b: (b, 0)),
                  pl.BlockSpec((CHUNK, CHUNK), lambda b: (0, 0))],
        out_specs=[pl.BlockSpec((seq, LANES), lambda b: (b, 0)), col, col],
        out_shape=[jax.ShapeDtypeStruct((bsz * seq, LANES), F32),
                   jax.ShapeDtypeStruct((bsz, H_FOX, seq), F32),
                   jax.ShapeDtypeStruct((bsz, H_FOX, seq), F32)],
        compiler_params=pltpu.CompilerParams(dimension_semantics=("arbitrary",)),
        name="cum",
    )(lfp, tri)


def _fox_prompt_kernel(q_ref, kt_ref, vt_ref, cum_ref, cumt_ref, o_ref, k_sc, vt_sc,
                       *, tq, prep_chunk, pairs):
    grp = pl.program_id(1)
    i = pl.program_id(2)
    seq = kt_ref.shape[1]
    nh = 2 * pairs
    head0 = grp * nh

    @pl.when(i == 0)
    def _():
        lane = lax.broadcasted_iota(jnp.int32, (prep_chunk, LANES), 1)
        for c in range(seq // prep_chunk):
            sl = slice(c * prep_chunk, (c + 1) * prep_chunk)
            vt_sc[:, sl] = vt_ref[:, sl].astype(BF16)
            cum = cum_ref[sl, :] * LOG2E
            for pr in range(pairs):
                aux = jnp.where(lane < 3, 1.0, 0.0)
                for hh in range(2):
                    col = jnp.sum(jnp.where(lane == head0 + 2 * pr + hh, cum, 0.0),
                                  axis=-1, keepdims=True)
                    for t, piece in enumerate(_split3(col)):
                        aux = jnp.where(lane == 3 + 3 * hh + t, -piece.astype(F32), aux)
                k_rows = kt_ref[pr * LANES:(pr + 1) * LANES, sl].T
                k_sc[pr, sl, :] = jnp.concatenate([k_rows, aux], axis=1).astype(BF16)

    qt = (q_ref[...] * (ATTN_SCALE * LOG2E)).T
    q0 = pl.multiple_of(i * tq, tq)
    row = lax.broadcasted_iota(jnp.int32, (LANES, tq), 0)
    q_aug = []
    for pr in range(pairs):
        q_pair = qt[pr * LANES:(pr + 1) * LANES]
        for hh in range(2):
            cq = cumt_ref[pl.ds(head0 + 2 * pr + hh, 1), pl.ds(q0, tq)] * LOG2E
            a, b, c = (x.astype(F32) for x in _split3(cq))
            selector = jnp.where((row >= 3 + 3 * hh) & (row < 6 + 3 * hh), 1.0, 0.0)
            bottom = jnp.where(row == 0, a, jnp.where(row == 1, b, jnp.where(row == 2, c, selector)))
            top = jnp.where(row // HEAD_DIM == hh, q_pair, 0.0)
            q_aug.append(jnp.concatenate([top, bottom], axis=0).astype(BF16))

    keep = (lax.broadcasted_iota(jnp.int32, (tq, tq), 0)
            <= lax.broadcasted_iota(jnp.int32, (tq, tq), 1))

    def attend(n_keys):
        below = n_keys - tq
        strips = []
        for pr in range(pairs):
            kb = k_sc[pr, 0:n_keys, :]
            strips += [_dot(kb, q_aug[2 * pr + hh]) for hh in range(2)]
        probs = []
        for s in strips:
            s_diag = jnp.where(keep, s[below:], -jnp.inf)
            m = jnp.max(s_diag, axis=0, keepdims=True)
            if below:
                m = jnp.maximum(m, jnp.max(s[:below], axis=0, keepdims=True))
            p_diag = jnp.exp2(s_diag - m)
            l = jnp.sum(p_diag, axis=0, keepdims=True)
            p = p_diag.astype(BF16)
            if below:
                p_below = jnp.exp2(s[:below] - m)
                l = l + jnp.sum(p_below, axis=0, keepdims=True)
                p = jnp.concatenate([p_below.astype(BF16), p], axis=0)
            probs.append((p, l))
        outs = []
        for h, (p, l) in enumerate(probs):
            vt = vt_sc[h * HEAD_DIM:(h + 1) * HEAD_DIM, 0:n_keys]
            outs.append(_dot(vt, p) / l)
        o_ref[...] = jnp.concatenate(outs, axis=0).T

    for blk in range(seq // tq):
        pl.when(i == blk)(functools.partial(attend, (blk + 1) * tq))


def _fox_prompt(fq, fkt, fvt, layer, cum, cumt, bsz, seq, tq, pairs):
    nq = seq // tq
    width = pairs * LANES
    groups = FOX_W // width
    prep_chunk = _pick_tile(seq, 512)
    kv = pl.BlockSpec((None, None, width, seq), lambda b, g, i: (layer, b, g, 0))
    return pl.pallas_call(
        functools.partial(_fox_prompt_kernel, tq=tq, prep_chunk=prep_chunk, pairs=pairs),
        grid=(bsz, groups, nq),
        in_specs=[pl.BlockSpec((tq, width), lambda b, g, i: (b * nq + i, g)),
                  kv, kv,
                  pl.BlockSpec((seq, LANES), lambda b, g, i: (b, 0)),
                  pl.BlockSpec((None, H_FOX, seq), lambda b, g, i: (b, 0, 0))],
        out_specs=pl.BlockSpec((tq, width), lambda b, g, i: (b * nq + i, g)),
        out_shape=jax.ShapeDtypeStruct((bsz * seq, FOX_W), F32),
        scratch_shapes=[pltpu.VMEM((pairs, seq, 2 * LANES), BF16), pltpu.VMEM((width, seq), BF16)],
        compiler_params=pltpu.CompilerParams(
            dimension_semantics=("arbitrary", "arbitrary", "arbitrary"),
            vmem_limit_bytes=VMEM_LIMIT),
        name="fox_prompt",
    )(fq, fkt, fvt, cum, cumt)


def _fox_sample_kernel(pt_ref, q_ref, kn_ref, vn_ref, lfn_ref, w_ref, *rest, pages_per_step):
    del pt_ref
    g = pages_per_step
    k_refs = rest[:g]
    v_refs = rest[g:2 * g]
    lf_refs = rest[2 * g:3 * g]
    o_ref = rest[3 * g]
    m_sc, l_sc, acc_sc, after_sc = rest[3 * g + 1:]
    jg = pl.program_id(1)

    sub = lax.broadcasted_iota(jnp.int32, (H_FOX, FOX_W), 0)
    lane = lax.broadcasted_iota(jnp.int32, (H_FOX, FOX_W), 1)
    hmask = (lane // HEAD_DIM) == sub
    qrows = jnp.where(hmask, q_ref[...] * ATTN_SCALE, 0.0)

    @pl.when(jg == 0)
    def _():
        m_sc[...] = jnp.full_like(m_sc, NEG_BIG)
        l_sc[...] = jnp.zeros_like(l_sc)
        acc_sc[...] = jnp.zeros_like(acc_sc)
        after_sc[...] = jnp.zeros_like(after_sc)

    qb = qrows.astype(BF16)
    w = w_ref[...]
    m = m_sc[...]
    l = l_sc[...]
    after = after_sc[...]
    lf_all = jnp.concatenate([r[...] for r in lf_refs], axis=0)
    a, b, c = _split3(lf_all)
    st_all = (_dot(a, w) + _dot(b, w)) + _dot(c, w)
    scores = []
    for t in range(g):
        st = st_all[t * H_FOX:(t + 1) * H_FOX]
        kb = k_refs[t][...].astype(BF16)
        scores.append(_dot(qb, kb) + st[:, :PAGE_SIZE] + after)
        after = after + st[:, PAGE_SIZE:]
    s_max = scores[0]
    for t in range(1, g):
        s_max = jnp.maximum(s_max, scores[t])
    m_new = jnp.maximum(m, jnp.max(s_max, axis=-1, keepdims=True))
    alpha = jnp.exp(m - m_new)
    p_sum = None
    pv = None
    for t in range(g):
        p = jnp.exp(scores[t] - m_new)
        p_sum = p if p_sum is None else p_sum + p
        d = _dot_nt(p.astype(BF16), v_refs[t][...].astype(BF16))
        pv = d if pv is None else pv + d
    l = alpha * l + jnp.sum(p_sum, axis=-1, keepdims=True)
    acc = alpha * acc_sc[...] + pv
    m = m_new
    m_sc[...] = m
    l_sc[...] = l
    acc_sc[...] = acc
    after_sc[...] = after

    @pl.when(jg == pl.num_programs(1) - 1)
    def _():
        s_new = jnp.sum(qrows * kn_ref[...], axis=-1, keepdims=True) - lfn_ref[...]
        m_fin = jnp.maximum(m, s_new)
        alpha = jnp.exp(m - m_fin)
        p_new = jnp.exp(s_new - m_fin)
        l_fin = alpha * l + p_new
        acc_fin = alpha * acc + p_new * vn_ref[...]
        out = jnp.where(hmask, acc_fin / l_fin, 0.0)
        o_ref[...] = jnp.sum(out, axis=0, keepdims=True)


def _fox_sample(fq, fk, fv, lf, ck_t, cv_t, clf_t, page_table, layer, w_sfx, g):
    bsz = fq.shape[0]
    n_pages = page_table.shape[1]
    n_steps = n_pages // g
    q3 = fq.reshape(bsz, 1, FOX_W)
    k3 = fk.reshape(bsz, 1, FOX_W)
    v3 = fv.reshape(bsz, 1, FOX_W)
    lf3 = lf.reshape(bsz, H_FOX, 1)
    row = pl.BlockSpec((None, 1, FOX_W), lambda b, j, pt: (b, 0, 0))

    def page_spec(rows, t):
        def imap(b, j, pt):
            return (layer, pt[b, n_pages - 1 - (j * g + t)], 0, 0)
        return pl.BlockSpec((None, None, rows, PAGE_SIZE), imap)

    in_specs = [row, row, row,
                pl.BlockSpec((None, H_FOX, 1), lambda b, j, pt: (b, 0, 0)),
                pl.BlockSpec(w_sfx.shape, lambda b, j, pt: (0, 0))]
    in_specs += [page_spec(FOX_W, t) for t in range(g)]
    in_specs += [page_spec(FOX_W, t) for t in range(g)]
    in_specs += [page_spec(H_FOX, t) for t in range(g)]
    out = pl.pallas_call(
        functools.partial(_fox_sample_kernel, pages_per_step=g),
        grid_spec=pltpu.PrefetchScalarGridSpec(
            num_scalar_prefetch=1,
            grid=(bsz, n_steps),
            in_specs=in_specs,
            out_specs=row,
            scratch_shapes=[pltpu.VMEM((H_FOX, 1), F32), pltpu.VMEM((H_FOX, 1), F32),
                            pltpu.VMEM((H_FOX, FOX_W), F32), pltpu.VMEM((H_FOX, PAGE_SIZE), F32)]),
        out_shape=jax.ShapeDtypeStruct((bsz, 1, FOX_W), F32),
        compiler_params=pltpu.CompilerParams(
            dimension_semantics=("arbitrary", "arbitrary"), vmem_limit_bytes=VMEM_LIMIT),
        name="fox_sample",
    )(page_table, q3, k3, v3, lf3, w_sfx, *([ck_t] * g), *([cv_t] * g), *([clf_t] * g))
    return out.reshape(bsz, FOX_W)


def _merge_out(x_ref, ro_ref, rg_ref, fo_ref, y_c, seg_ref, retg_ref, wout_ref):
    y_ret = jax.nn.silu(rg_ref[...]) * _head_rms(ro_ref[...], seg_ref[...], retg_ref[...])
    y = jnp.concatenate([y_ret, fo_ref[...], y_c], axis=-1).astype(BF16)
    return x_ref[...] + _dot(y, wout_ref[...])


def _merge_prompt(x_ref, ro_ref, rg_ref, fo_ref, cu_ref, cv_ref, seg_ref, retg_ref,
                  sw_ref, sb_ref, wout_ref):
    tm = x_ref.shape[0]
    r = lax.broadcasted_iota(jnp.int32, (CHUNK, CHUNK), 0)
    c = lax.broadcasted_iota(jnp.int32, (CHUNK, CHUNK), 1)
    lane = lax.broadcasted_iota(jnp.int32, (CHUNK, CMLP_W), 1)
    ws = [jnp.where(r >= c, sw_ref[g], 0.0).astype(BF16) for g in range(C_GROUPS)]
    sb = sb_ref[...]
    chunks = []
    for n in range(tm // CHUNK):
        vb = cv_ref[n * CHUNK:(n + 1) * CHUNK, :].astype(BF16)
        mixed = sb
        for g in range(C_GROUPS):
            mixed = mixed + jnp.where((lane // HEAD_DIM) == g, _dot(ws[g], vb), 0.0)
        chunks.append(cu_ref[n * CHUNK:(n + 1) * CHUNK, :] * mixed)
    y_c = chunks[0] if len(chunks) == 1 else jnp.concatenate(chunks, axis=0)
    return _merge_out(x_ref, ro_ref, rg_ref, fo_ref, y_c, seg_ref, retg_ref, wout_ref)


def _merge_sample_kernel(x_ref, ro_ref, rg_ref, fo_ref, cu_ref, cv_ref, seg_ref, retg_ref,
                         sw0_ref, sb0_ref, wout_ref, out_ref):
    y_c = cu_ref[...] * (sw0_ref[...] * cv_ref[...] + sb0_ref[...])
    out_ref[...] = _merge_out(x_ref, ro_ref, rg_ref, fo_ref, y_c, seg_ref, retg_ref, wout_ref)


def _merge_specs(tm, d, wout, sample, row, const):
    if sample:
        sw_spec = pl.BlockSpec((1, CMLP_W), const)
        sb_spec = pl.BlockSpec((1, CMLP_W), const)
    else:
        sw_spec = pl.BlockSpec((C_GROUPS, CHUNK, CHUNK), lambda i: (0, 0, 0))
        sb_spec = pl.BlockSpec((CHUNK, CMLP_W), const)
    return [pl.BlockSpec((tm, d), row),
            pl.BlockSpec((tm, RET_W), row), pl.BlockSpec((tm, RET_W), row),
            pl.BlockSpec((tm, FOX_W), row),
            pl.BlockSpec((tm, CMLP_W), row), pl.BlockSpec((tm, CMLP_W), row),
            pl.BlockSpec((256, 256), const), pl.BlockSpec((1, RET_W), const),
            sw_spec, sb_spec,
            pl.BlockSpec(wout.shape, const)]


def _merge_sample(x2d, ro, rg, fo, cu, cv, seg, retg, sw0, sb0, wout):
    n, d = x2d.shape
    row = lambda i: (i, 0)
    const = lambda i: (0, 0)
    return pl.pallas_call(
        _merge_sample_kernel,
        grid=(1,),
        in_specs=_merge_specs(n, d, wout, True, row, const),
        out_specs=pl.BlockSpec((n, d), row),
        out_shape=jax.ShapeDtypeStruct((n, d), F32),
        compiler_params=pltpu.CompilerParams(
            dimension_semantics=("arbitrary",), vmem_limit_bytes=VMEM_LIMIT),
        name="merge_sample",
    )(x2d, ro, rg, fo, cu, cv, seg, retg, sw0, sb0, wout)


def _ffn_prompt_kernel(*refs, tiles_per_seq, fb):
    merge_refs = refs[:11]
    g2_ref, wu_ref, cw_ref, cb_ref, wd_ref, prev_ref, out_ref, st_ref, act_sc, carry_sc = refs[11:]
    t = pl.program_id(0) % tiles_per_seq
    tm = out_ref.shape[0]
    d_ff = wd_ref.shape[0]

    @pl.when(t == 0)
    def _():
        carry_sc[...] = prev_ref[...]

    x = _merge_prompt(*merge_refs)
    hb = _row_rms(x, g2_ref[...]).astype(BF16)
    row = lax.broadcasted_iota(jnp.int32, (tm, fb), 0)

    def conv_cols(lo):
        u = _dot(hb, wu_ref[:, lo:lo + fb])
        prev = carry_sc[:, lo:lo + fb]
        s1 = jnp.where(row == 0, prev[1:2], pltpu.roll(u, 1, 0))
        s2 = jnp.where(row == 0, prev[0:1], jnp.where(row == 1, prev[1:2], pltpu.roll(u, 2, 0)))
        carry_sc[:, lo:lo + fb] = u[tm - (CONV_W - 1):tm, :]
        return (((cb_ref[:, lo:lo + fb] + cw_ref[0:1, lo:lo + fb] * s2)
                 + cw_ref[1:2, lo:lo + fb] * s1) + cw_ref[2:3, lo:lo + fb] * u)

    for c in range(d_ff // fb):
        gate = conv_cols(c * fb)
        val = conv_cols(d_ff + c * fb)
        act_sc[:, c * fb:(c + 1) * fb] = (jax.nn.silu(gate) * val).astype(BF16)
    st_ref[...] = carry_sc[...]
    out_ref[...] = x + _dot(act_sc[...], wd_ref[...])


def _ffn_sample_kernel(x_ref, g2_ref, wug_ref, wuv_ref, cwg_ref, cwv_ref, cbg_ref, cbv_ref, wd_ref,
                       p0g_ref, p0v_ref, p1g_ref, p1v_ref, out_ref, st_ref, h_sc, acc_sc, *, d_ff):
    j = pl.program_id(1)
    fb = wug_ref.shape[1]

    @pl.when(j == 0)
    def _():
        h_sc[...] = _row_rms(x_ref[...], g2_ref[...]).astype(BF16)
        acc_sc[...] = jnp.zeros_like(acc_sc)

    hb = h_sc[...]

    def conv(u, p0, p1, w_ref, b_ref):
        return ((b_ref[...] + w_ref[0:1] * p0) + w_ref[1:2] * p1) + w_ref[2:3] * u

    ug = _dot(hb, wug_ref[...])
    uv = _dot(hb, wuv_ref[...])
    p1g = p1g_ref[...]
    p1v = p1v_ref[...]
    conv_g = conv(ug, p0g_ref[...], p1g, cwg_ref, cbg_ref)
    conv_v = conv(uv, p0v_ref[...], p1v, cwv_ref, cbv_ref)
    cg = pl.multiple_of(j * fb, LANES)
    cv = pl.multiple_of(d_ff + j * fb, LANES)
    st_ref[:, pl.ds(cg, fb)] = p1g
    st_ref[:, pl.ds(cv, fb)] = p1v
    st_ref[:, pl.ds(pl.multiple_of(2 * d_ff + j * fb, LANES), fb)] = ug
    st_ref[:, pl.ds(pl.multiple_of(3 * d_ff + j * fb, LANES), fb)] = uv
    act = (jax.nn.silu(conv_g) * conv_v).astype(BF16)
    acc_sc[...] += _dot(act, wd_ref[...])

    @pl.when(j == pl.num_programs(1) - 1)
    def _():
        out_ref[...] = x_ref[...] + acc_sc[...]


def _ffn(x2d, g2, wup, cw, cb, wdown, prev, tm, fb, seq, merge_args=None):
    n, d = x2d.shape
    d_ff = wdown.shape[0]
    nff = d_ff // fb
    nt = n // tm
    xspec = pl.BlockSpec((tm, d), lambda i, j: (i, 0))
    common = [xspec,
              pl.BlockSpec((1, d), lambda i, j: (0, 0)),
              pl.BlockSpec((d, fb), lambda i, j: (0, j)),
              pl.BlockSpec((d, fb), lambda i, j: (0, nff + j)),
              pl.BlockSpec((CONV_W, fb), lambda i, j: (0, j)),
              pl.BlockSpec((CONV_W, fb), lambda i, j: (0, nff + j)),
              pl.BlockSpec((1, fb), lambda i, j: (0, j)),
              pl.BlockSpec((1, fb), lambda i, j: (0, nff + j)),
              pl.BlockSpec((fb, d), lambda i, j: (j, 0))]
    scratch = [pltpu.VMEM((tm, d), BF16), pltpu.VMEM((tm, d), F32)]
    params = pltpu.CompilerParams(
        dimension_semantics=("arbitrary", "arbitrary"), vmem_limit_bytes=VMEM_LIMIT)
    if seq == 1:
        bsz = n
        prev2 = prev.reshape(bsz, (CONV_W - 1) * 2 * d_ff)
        pspecs = [pl.BlockSpec((bsz, fb), lambda i, j, o=o: (0, o + j))
                  for o in (0, nff, 2 * nff, 3 * nff)]
        out, st = pl.pallas_call(
            functools.partial(_ffn_sample_kernel, d_ff=d_ff),
            grid=(nt, nff),
            in_specs=common + [pspecs[0], pspecs[1], pspecs[2], pspecs[3]],
            out_specs=[xspec, pl.BlockSpec(prev2.shape, lambda i, j: (0, 0))],
            out_shape=[jax.ShapeDtypeStruct((n, d), F32), jax.ShapeDtypeStruct(prev2.shape, F32)],
            scratch_shapes=scratch,
            compiler_params=params,
            name="ffn_sample",
        )(x2d, g2, wup, wup, cw, cw, cb, cb, wdown, prev2, prev2, prev2, prev2)
        return out, st.reshape(prev.shape)
    tiles_per_seq = seq // tm
    const = lambda i: (0, 0)
    resident = functools.partial(pl.BlockSpec, pipeline_mode=pl.Buffered(1))
    state = pl.BlockSpec((None, CONV_W - 1, 2 * d_ff), lambda i: (i // tiles_per_seq, 0, 0))
    row = pl.BlockSpec((tm, d), lambda i: (i, 0))
    wout = merge_args[-1]
    merge_specs = _merge_specs(tm, d, wout, False, lambda i: (i, 0), const)
    merge_specs[-1] = resident(wout.shape, const)
    return pl.pallas_call(
        functools.partial(_ffn_prompt_kernel, tiles_per_seq=tiles_per_seq, fb=fb),
        grid=(nt,),
        in_specs=merge_specs + [
            pl.BlockSpec((1, d), const),
            resident(wup.shape, const),
            pl.BlockSpec(cw.shape, const),
            pl.BlockSpec(cb.shape, const),
            resident(wdown.shape, const),
            state],
        out_specs=[row, state],
        out_shape=[jax.ShapeDtypeStruct((n, d), F32), jax.ShapeDtypeStruct(prev.shape, F32)],
        scratch_shapes=[pltpu.VMEM((tm, d_ff), BF16), pltpu.VMEM((CONV_W - 1, 2 * d_ff), F32)],
        compiler_params=pltpu.CompilerParams(
            dimension_semantics=("arbitrary",), vmem_limit_bytes=VMEM_LIMIT),
        name="ffn_prompt",
    )(x2d, *merge_args, g2, wup, cw, cb, wdown, prev)


def _rope_tables(pos):
    half = HEAD_DIM // 2
    inv = 1.0 / (ROPE_BASE ** (jnp.arange(half, dtype=F32) / half))
    ang = pos.astype(F32)[:, None] * inv[None, :]
    cos = jnp.cos(ang)
    sin = jnp.sin(ang)
    cos_e = jnp.tile(jnp.concatenate([cos, cos], axis=-1), (1, H_RET))
    sin_e = jnp.tile(jnp.concatenate([-sin, sin], axis=-1), (1, H_RET))
    return cos_e, sin_e


def _retention_tables():
    c = CHUNK
    lg = jnp.log1p(-jnp.exp2(-5.0 - jnp.arange(H_RET, dtype=F32)))
    idx = jnp.arange(c, dtype=F32)
    diff = idx[:, None] - idx[None, :]
    intra = jnp.where(diff[None] >= 0, jnp.exp(jnp.maximum(diff, 0.0)[None] * lg[:, None, None]), 0.0)
    q_dec = jnp.exp((idx[None, :] + 1.0) * lg[:, None])
    k_dec = jnp.exp((c - 1.0 - idx[None, :]) * lg[:, None])
    chunk_dec = jnp.exp(c * lg)
    expand = lambda a: jnp.repeat(a.T, HEAD_DIM, axis=1)
    head_of = jnp.arange(RET_W) // HEAD_DIM
    bd = (head_of[:, None] == head_of[None, :]).astype(F32)
    decmat = jnp.broadcast_to(chunk_dec[head_of][:, None], (RET_W, RET_W))
    gamma = jnp.exp(1.0 * lg)
    gcol = jnp.broadcast_to(gamma[:, None, None], (H_RET, HEAD_DIM, 1))
    grow = jnp.broadcast_to(gamma[:, None, None], (H_RET, 1, HEAD_DIM))
    return dict(intra=intra, qdec=expand(q_dec), kdec=expand(k_dec), decmat=decmat, bd=bd,
                gcol=gcol, grow=grow)


def _pick_tile(n, target):
    t = min(n, target)
    while n % t:
        t //= 2
    return t


@jax.jit
def kernel(x_prompt, x_sample, cache_k, cache_v, cache_logf, state_ret, state_ffn_conv, page_table,
           norm1_g, w_in, b_forget, ret_norm_g, q_norm_g, k_norm_g, spatial_w, spatial_b, w_out,
           norm2_g, w_up, conv_w, conv_b, w_down):
    bp, lp, d = x_prompt.shape
    bs, ls, _ = x_sample.shape
    depth = w_in.shape[0]
    d_ff = w_down.shape[1]
    n_pages = page_table.shape[1]
    past_len = n_pages * PAGE_SIZE
    assert ls == 1 and lp % CHUNK == 0

    fz0 = 4 * RET_W + 3 * FOX_W
    w_in_p = jnp.concatenate(
        [w_in[..., :fz0], w_in[..., fz0 + H_FOX:], w_in[..., fz0:fz0 + H_FOX],
         jnp.zeros((depth, d, FZ_COLS - H_FOX), w_in.dtype)], axis=-1).astype(BF16)
    w_out_b = w_out.astype(BF16)
    w_up_b = w_up.astype(BF16)
    w_down_b = w_down.astype(BF16)
    bf_p = jnp.pad(b_forget, ((0, 0), (0, FZ_COLS - H_FOX)))
    qg_e = jnp.tile(q_norm_g, (1, H_FOX))
    kg_e = jnp.tile(k_norm_g, (1, H_FOX))
    retg_e = ret_norm_g.reshape(depth, RET_W)
    sb_e = jnp.repeat(jnp.swapaxes(spatial_b, 1, 2), HEAD_DIM, axis=2)
    sw0_e = jnp.repeat(spatial_w[:, :, 0, 0], HEAD_DIM, axis=1)
    sb0_e = jnp.repeat(spatial_b[:, :, 0], HEAD_DIM, axis=1)

    head_of = jnp.arange(256) // HEAD_DIM
    seg = (head_of[:, None] == head_of[None, :]).astype(BF16)
    idx = jnp.arange(CHUNK)
    tri_incl = (idx[:, None] >= idx[None, :]).astype(BF16)
    sfx = (idx[:, None] > idx[None, :]).astype(BF16)
    w_sfx = jnp.concatenate([sfx, jnp.ones((PAGE_SIZE, PAGE_SIZE), BF16)], axis=1)
    ck_t = jnp.transpose(cache_k, (0, 1, 3, 4, 2)).reshape(depth, -1, FOX_W, PAGE_SIZE)
    cv_t = jnp.transpose(cache_v, (0, 1, 3, 4, 2)).reshape(depth, -1, FOX_W, PAGE_SIZE)
    clf_t = jnp.transpose(cache_logf, (0, 1, 3, 2))
    cos_p, sin_p = _rope_tables(jnp.arange(lp))
    cos_s, sin_s = _rope_tables(jnp.broadcast_to(past_len + jnp.arange(ls), (bs,)))
    tabs = _retention_tables()

    tm_p = _pick_tile(lp, 512)
    tq = _pick_tile(lp, 256)
    tm_f = _pick_tile(lp, 512)
    fb = 256
    g_pages = _pick_tile(n_pages, 16)
    zeros_prev = jnp.zeros((bp, CONV_W - 1, 2 * d_ff), F32)

    xp = x_prompt.reshape(bp * lp, d)
    xs = x_sample.reshape(bs * ls, d)
    outs = [[] for _ in range(11)]
    fkt = fvt = None
    for l in range(depth):
        g1 = norm1_g[l][None]
        g2 = norm2_g[l][None]
        bfl = bf_p[l][None]
        qg = qg_e[l][None]
        kg = kg_e[l][None]
        retg = retg_e[l][None]
        cb = conv_b[l][None]

        rq, rk, rv, rg, fq, fkt, fvt, _, lfp, cu, cv = _proj(
            xp, cos_p, sin_p, g1, w_in_p[l], bfl, qg, kg, seg, tm_p, (depth, l, fkt, fvt))
        ro, r_state = _ret_prompt(rq, rk, rv, tabs, bp, lp)
        cum, cumt, lft = _cum(lfp, tri_incl, bp, lp)
        fo = _fox_prompt(fq, fkt, fvt, l, cum, cumt, bp, lp, tq, 2)
        xp, c_state = _ffn(xp, g2, w_up_b[l], conv_w[l], cb, w_down_b[l], zeros_prev, tm_f, fb, lp,
                           (ro, rg, fo, cu, cv, seg, retg, spatial_w[l], sb_e[l], w_out_b[l]))
        outs[2].append(lft)
        outs[3].append(r_state)
        outs[4].append(c_state)

        rq, rk, rv, rg, fq, fk, fv, lf, lfp, cu, cv = _proj(
            xs, cos_s, sin_s, g1, w_in_p[l], bfl, qg, kg, seg, bs)
        ro, r_state = _ret_sample(rq, rk, rv, state_ret, l, tabs)
        fo = _fox_sample(fq, fk, fv, lf, ck_t, cv_t, clf_t, page_table, l, w_sfx, g_pages)
        xs = _merge_sample(xs, ro, rg, fo, cu, cv, seg, retg, sw0_e[l][None], sb0_e[l][None],
                           w_out_b[l])
        xs, c_state = _ffn(xs, g2, w_up_b[l], conv_w[l], cb, w_down_b[l], state_ffn_conv[l], bs, fb, 1)
        outs[5].append(fk.reshape(bs, ls, H_FOX, HEAD_DIM))
        outs[6].append(fv.reshape(bs, ls, H_FOX, HEAD_DIM))
        outs[7].append(lf.reshape(bs, ls, H_FOX))
        outs[8].append(r_state)
        outs[9].append(cv.reshape(bs, ls, CMLP_W))
        outs[10].append(c_state)

    stacked = [jnp.stack(o) if o else None for o in outs]
    for o, buf in ((0, fkt), (1, fvt)):
        stacked[o] = jnp.transpose(buf.reshape(depth, bp, H_FOX, HEAD_DIM, lp), (0, 1, 4, 2, 3))
    stacked[2] = jnp.transpose(stacked[2], (0, 1, 3, 2))
    return (xp.reshape(bp, lp, d), xs.reshape(bs, ls, d), *stacked)
```

```python
import functools

import jax
import jax.numpy as jnp
from jax import lax
from jax.experimental import pallas as pl
from jax.experimental.pallas import tpu as pltpu

HEAD_DIM = 64
H_RET = 4
H_FOX = 8
C_GROUPS = 4
RET_W = H_RET * HEAD_DIM
FOX_W = H_FOX * HEAD_DIM
CMLP_W = C_GROUPS * HEAD_DIM
CHUNK = 128
PAGE_SIZE = 128
CONV_W = 3
RMS_EPS = 1e-6
ROPE_BASE = 10000.0
ATTN_SCALE = HEAD_DIM ** -0.5

LANES = 128
FZ_COLS = LANES
VMEM_LIMIT = 56 * 1024 * 1024
NEG_BIG = -1e30
LOG2E = 1.4426950408889634

F32 = jnp.float32
BF16 = jnp.bfloat16


def _dot(a, b):
    return jnp.dot(a, b, preferred_element_type=F32)


def _dot_nt(a, b):
    return lax.dot_general(a, b, (((1,), (1,)), ((), ())), preferred_element_type=F32)


def _split2(x):
    hi = x.astype(BF16)
    lo = (x - hi.astype(F32)).astype(BF16)
    return hi, lo


def _split3(x):
    a = x.astype(BF16)
    r = x - a.astype(F32)
    b = r.astype(BF16)
    c = (r - b.astype(F32)).astype(BF16)
    return a, b, c


def _seg_mean(sq, seg):
    hi, lo = _split2(sq)
    return (_dot(hi, seg) + _dot(lo, seg)) * (1.0 / HEAD_DIM)


def _head_rms(x, seg, g):
    parts = []
    for c in range(x.shape[-1] // 256):
        xc = x[:, c * 256:(c + 1) * 256]
        parts.append(xc * lax.rsqrt(_seg_mean(xc * xc, seg) + RMS_EPS))
    y = parts[0] if len(parts) == 1 else jnp.concatenate(parts, axis=-1)
    return y * g


def _row_rms(x, g):
    return x * lax.rsqrt(jnp.mean(x * x, axis=-1, keepdims=True) + RMS_EPS) * g


def _proj_kernel(x_ref, cos_ref, sin_ref, g1_ref, w_ref, bf_ref, qg_ref, kg_ref, seg_ref, *refs,
                 kv_transposed):
    (rq_ref, rk_ref, rv_ref, rg_ref, fq_ref, fk_ref, fv_ref, lf_ref, lfp_ref,
     cu_ref, cv_ref) = refs[-11:]
    hb = _row_rms(x_ref[...], g1_ref[...]).astype(BF16)
    seg = seg_ref[...]

    def proj(lo, n):
        return _dot_nt(hb, w_ref[lo:lo + n, :])

    cos = cos_ref[...]
    sin = sin_ref[...]
    lane = lax.broadcasted_iota(jnp.int32, cos.shape, 1)
    first_half = (lane % HEAD_DIM) < (HEAD_DIM // 2)

    def rot(z):
        swapped = jnp.where(first_half, pltpu.roll(z, RET_W - HEAD_DIM // 2, 1),
                            pltpu.roll(z, HEAD_DIM // 2, 1))
        return z * cos + swapped * sin

    rq_ref[...] = rot(proj(0, RET_W))
    rk_ref[...] = rot(proj(RET_W, RET_W)) * ATTN_SCALE
    rv_ref[...] = proj(2 * RET_W, RET_W)
    rg_ref[...] = proj(3 * RET_W, RET_W)
    off = 4 * RET_W
    fq_ref[...] = _head_rms(proj(off, FOX_W), seg, qg_ref[...])
    fk = _head_rms(proj(off + FOX_W, FOX_W), seg, kg_ref[...])
    fv = proj(off + 2 * FOX_W, FOX_W)
    fk_ref[...] = fk.T if kv_transposed else fk
    fv_ref[...] = fv.T if kv_transposed else fv
    off += 3 * FOX_W
    cu_ref[...] = jax.nn.gelu(proj(off, CMLP_W))
    cv_ref[...] = jax.nn.gelu(proj(off + CMLP_W, CMLP_W))
    off += 2 * CMLP_W
    lf = jax.nn.log_sigmoid(proj(off, FZ_COLS) + bf_ref[...])
    lane_z = lax.broadcasted_iota(jnp.int32, lf.shape, 1)
    lfp_ref[...] = jnp.where(lane_z < H_FOX, lf, 0.0)
    lf_ref[...] = lf[:, :H_FOX]


def _proj(x2d, cos, sin, g1, w, bfp, qg, kg, seg, tm, kv_step=None):
    kv_transposed = kv_step is not None
    n, d = x2d.shape
    nt = n // tm
    tt = cos.shape[0] // tm
    row = lambda i: (i, 0)
    const = lambda i: (0, 0)
    outs = [RET_W] * 4 + [FOX_W] * 3 + [H_FOX, FZ_COLS] + [CMLP_W] * 2
    out_specs = [pl.BlockSpec((tm, c), row) for c in outs]
    out_shape = [jax.ShapeDtypeStruct((n, c), F32) for c in outs]
    in_specs = [
        pl.BlockSpec((tm, d), row),
        pl.BlockSpec((tm, RET_W), lambda i: (i % tt, 0)),
        pl.BlockSpec((tm, RET_W), lambda i: (i % tt, 0)),
        pl.BlockSpec((1, d), const),
        pl.BlockSpec(w.shape, const),
        pl.BlockSpec((1, FZ_COLS), const),
        pl.BlockSpec((1, FOX_W), const),
        pl.BlockSpec((1, FOX_W), const),
        pl.BlockSpec((256, 256), const),
    ]
    args = [x2d, cos, sin, g1, w, bfp, qg, kg, seg]
    aliases = {}
    if kv_transposed:
        depth, layer, k_buf, v_buf = kv_step
        for o in (5, 6):
            out_specs[o] = pl.BlockSpec((None, None, FOX_W, tm),
                                        lambda i: (layer, i // tt, 0, i % tt))
            out_shape[o] = jax.ShapeDtypeStruct((depth, nt // tt, FOX_W, tt * tm), F32)
        if k_buf is not None:
            aliases = {len(args): 5, len(args) + 1: 6}
            in_specs += [pl.BlockSpec(memory_space=pl.ANY)] * 2
            args += [k_buf, v_buf]
    return pl.pallas_call(
        functools.partial(_proj_kernel, kv_transposed=kv_transposed),
        grid=(nt,),
        in_specs=in_specs,
        out_specs=out_specs,
        out_shape=out_shape,
        input_output_aliases=aliases,
        compiler_params=pltpu.CompilerParams(
            dimension_semantics=("arbitrary",), vmem_limit_bytes=VMEM_LIMIT),
        name="proj",
    )(*args)


def _ret_prompt_kernel(q_ref, k_ref, v_ref, intra_ref, qdec_ref, kdec_ref, dec_ref, bd_ref,
                       o_ref, st_ref, s_sc, *, chunks):
    c = pl.program_id(1)

    @pl.when(c == 0)
    def _():
        s_sc[...] = jnp.zeros_like(s_sc)

    lane = lax.broadcasted_iota(jnp.int32, (CHUNK, RET_W), 1)
    state = s_sc[...]
    for n in range(chunks):
        sl = slice(n * CHUNK, (n + 1) * CHUNK)
        q = q_ref[sl, :]
        k = k_ref[sl, :]
        vb = v_ref[sl, :].astype(BF16)
        kb = k.astype(BF16)
        o = _dot((q * qdec_ref[...]).astype(BF16), state.astype(BF16))
        for h in range(H_RET):
            hm = (lane // HEAD_DIM) == h
            sc = _dot_nt(jnp.where(hm, q, 0.0).astype(BF16), kb) * intra_ref[h]
            o = o + jnp.where(hm, _dot(sc.astype(BF16), vb), 0.0)
        o_ref[sl, :] = o
        kd = (k * kdec_ref[...]).astype(BF16)
        ktv = lax.dot_general(kd, vb, (((0,), (0,)), ((), ())), preferred_element_type=F32)
        state = state * dec_ref[...] + bd_ref[...] * ktv
    s_sc[...] = state

    @pl.when(c == pl.num_programs(1) - 1)
    def _():
        for h in range(H_RET):
            lo = h * HEAD_DIM
            st_ref[h] = state[lo:lo + HEAD_DIM, lo:lo + HEAD_DIM]


def _ret_prompt(rq, rk, rv, tabs, bsz, seq):
    chunks = _pick_tile(seq // CHUNK, 4)
    rows = chunks * CHUNK
    ns = seq // rows
    blk = pl.BlockSpec((rows, RET_W), lambda b, c: (b * ns + c, 0))
    c2 = lambda b, c: (0, 0)
    return pl.pallas_call(
        functools.partial(_ret_prompt_kernel, chunks=chunks),
        grid=(bsz, ns),
        in_specs=[blk, blk, blk,
                  pl.BlockSpec((H_RET, CHUNK, CHUNK), lambda b, c: (0, 0, 0)),
                  pl.BlockSpec((CHUNK, RET_W), c2),
                  pl.BlockSpec((CHUNK, RET_W), c2),
                  pl.BlockSpec((RET_W, RET_W), c2),
                  pl.BlockSpec((RET_W, RET_W), c2)],
        out_specs=[blk, pl.BlockSpec((None, H_RET, HEAD_DIM, HEAD_DIM), lambda b, c: (b, 0, 0, 0))],
        out_shape=[jax.ShapeDtypeStruct((bsz * seq, RET_W), F32),
                   jax.ShapeDtypeStruct((bsz, H_RET, HEAD_DIM, HEAD_DIM), F32)],
        scratch_shapes=[pltpu.VMEM((RET_W, RET_W), F32)],
        compiler_params=pltpu.CompilerParams(
            dimension_semantics=("arbitrary", "arbitrary"), vmem_limit_bytes=VMEM_LIMIT),
        name="ret_prompt",
    )(rq, rk, rv, tabs["intra"], tabs["qdec"], tabs["kdec"], tabs["decmat"], tabs["bd"])


def _ret_sample_kernel(q_ref, k_ref, v_ref, s_ref, gcol_ref, grow_ref, o_ref, st_ref):
    for b in range(q_ref.shape[0]):
        for h in range(H_RET):
            qc = q_ref[b, h]
            kc = k_ref[b, h]
            vr = v_ref[b, h]
            s = s_ref[b, h]
            qk = jnp.sum(qc * kc, axis=0, keepdims=True)
            inter = jnp.sum((qc * gcol_ref[h]) * s, axis=0, keepdims=True)
            o_ref[b, h] = qk * vr + inter
            st_ref[b, h] = s * grow_ref[h] + kc * vr


def _ret_sample(rq, rk, rv, state, layer, tabs):
    bsz = rq.shape[0]
    nb = _pick_tile(bsz, 8)
    qc = rq.reshape(bsz, H_RET, HEAD_DIM, 1)
    kc = rk.reshape(bsz, H_RET, HEAD_DIM, 1)
    vr = rv.reshape(bsz, H_RET, 1, HEAD_DIM)
    col = pl.BlockSpec((nb, H_RET, HEAD_DIM, 1), lambda b: (b, 0, 0, 0))
    rowb = pl.BlockSpec((nb, H_RET, 1, HEAD_DIM), lambda b: (b, 0, 0, 0))
    stb = pl.BlockSpec((nb, H_RET, HEAD_DIM, HEAD_DIM), lambda b: (b, 0, 0, 0))
    o, st = pl.pallas_call(
        _ret_sample_kernel,
        grid=(bsz // nb,),
        in_specs=[col, col, rowb,
                  pl.BlockSpec((None, nb, H_RET, HEAD_DIM, HEAD_DIM), lambda b: (layer, b, 0, 0, 0)),
                  pl.BlockSpec((H_RET, HEAD_DIM, 1), lambda b: (0, 0, 0)),
                  pl.BlockSpec((H_RET, 1, HEAD_DIM), lambda b: (0, 0, 0))],
        out_specs=[rowb, stb],
        out_shape=[jax.ShapeDtypeStruct((bsz, H_RET, 1, HEAD_DIM), F32),
                   jax.ShapeDtypeStruct((bsz, H_RET, HEAD_DIM, HEAD_DIM), F32)],
        compiler_params=pltpu.CompilerParams(dimension_semantics=("arbitrary",)),
        name="ret_sample",
    )(qc, kc, vr, state, tabs["gcol"], tabs["grow"])
    return o.reshape(bsz, RET_W), st


def _cum_kernel(lf_ref, tri_ref, cum_ref, cumt_ref, lft_ref):
    tri = tri_ref[...]
    carry = jnp.zeros((1, LANES), F32)
    for c in range(lf_ref.shape[0] // CHUNK):
        sl = slice(c * CHUNK, (c + 1) * CHUNK)
        x = lf_ref[sl, :]
        a, b, r = _split3(x)
        cc = (_dot(tri, a) + _dot(tri, b)) + _dot(tri, r) + carry
        cum_ref[sl, :] = cc
        cumt_ref[:, sl] = cc.T[:H_FOX, :]
        lft_ref[:, sl] = x.T[:H_FOX, :]
        carry = cc[CHUNK - 1:CHUNK, :]


def _cum(lfp, tri, bsz, seq):
    col = pl.BlockSpec((None, H_FOX, seq), lambda b: (b, 0, 0))
    return pl.pallas_call(
        _cum_kernel,
        grid=(bsz,),
        in_specs=[pl.BlockSpec((seq, LANES), lambda b: (b, 0)),
                  pl.BlockSpec((CHUNK, CHUNK), lambda b: (0, 0))],
        out_specs=[pl.BlockSpec((seq, LANES), lambda b: (b, 0)), col, col],
        out_shape=[jax.ShapeDtypeStruct((bsz * seq, LANES), F32),
                   jax.ShapeDtypeStruct((bsz, H_FOX, seq), F32),
                   jax.ShapeDtypeStruct((bsz, H_FOX, seq), F32)],
        compiler_params=pltpu.CompilerParams(dimension_semantics=("arbitrary",)),
        name="cum",
    )(lfp, tri)


def _fox_prompt_kernel(q_ref, kt_ref, vt_ref, cum_ref, cumt_ref, o_ref, k_sc, vt_sc,
                       *, tq, prep_chunk, pairs):
    grp = pl.program_id(1)
    i = pl.program_id(2)
    seq = kt_ref.shape[1]
    nh = 2 * pairs
    head0 = grp * nh

    @pl.when(i == 0)
    def _():
        lane = lax.broadcasted_iota(jnp.int32, (prep_chunk, LANES), 1)
        for c in range(seq // prep_chunk):
            sl = slice(c * prep_chunk, (c + 1) * prep_chunk)
            vt_sc[:, sl] = vt_ref[:, sl].astype(BF16)
            cum = cum_ref[sl, :] * LOG2E
            for pr in range(pairs):
                aux = jnp.where(lane < 3, 1.0, 0.0)
                for hh in range(2):
                    col = jnp.sum(jnp.where(lane == head0 + 2 * pr + hh, cum, 0.0),
                                  axis=-1, keepdims=True)
                    for t, piece in enumerate(_split3(col)):
                        aux = jnp.where(lane == 3 + 3 * hh + t, -piece.astype(F32), aux)
                k_rows = kt_ref[pr * LANES:(pr + 1) * LANES, sl].T
                k_sc[pr, sl, :] = jnp.concatenate([k_rows, aux], axis=1).astype(BF16)

    qt = (q_ref[...] * (ATTN_SCALE * LOG2E)).T
    q0 = pl.multiple_of(i * tq, tq)
    row = lax.broadcasted_iota(jnp.int32, (LANES, tq), 0)
    q_aug = []
    for pr in range(pairs):
        q_pair = qt[pr * LANES:(pr + 1) * LANES]
        for hh in range(2):
            cq = cumt_ref[pl.ds(head0 + 2 * pr + hh, 1), pl.ds(q0, tq)] * LOG2E
            a, b, c = (x.astype(F32) for x in _split3(cq))
            selector = jnp.where((row >= 3 + 3 * hh) & (row < 6 + 3 * hh), 1.0, 0.0)
            bottom = jnp.where(row == 0, a, jnp.where(row == 1, b, jnp.where(row == 2, c, selector)))
            top = jnp.where(row // HEAD_DIM == hh, q_pair, 0.0)
            q_aug.append(jnp.concatenate([top, bottom], axis=0).astype(BF16))

    keep = (lax.broadcasted_iota(jnp.int32, (tq, tq), 0)
            <= lax.broadcasted_iota(jnp.int32, (tq, tq), 1))

    def attend(n_keys):
        below = n_keys - tq
        strips = []
        for pr in range(pairs):
            kb = k_sc[pr, 0:n_keys, :]
            strips += [_dot(kb, q_aug[2 * pr + hh]) for hh in range(2)]
        probs = []
        for s in strips:
            s_diag = jnp.where(keep, s[below:], -jnp.inf)
            m = jnp.max(s_diag, axis=0, keepdims=True)
            if below:
                m = jnp.maximum(m, jnp.max(s[:below], axis=0, keepdims=True))
            p_diag = jnp.exp2(s_diag - m)
            l = jnp.sum(p_diag, axis=0, keepdims=True)
            p = p_diag.astype(BF16)
            if below:
                p_below = jnp.exp2(s[:below] - m)
                l = l + jnp.sum(p_below, axis=0, keepdims=True)
                p = jnp.concatenate([p_below.astype(BF16), p], axis=0)
            probs.append((p, l))
        outs = []
        for h, (p, l) in enumerate(probs):
            vt = vt_sc[h * HEAD_DIM:(h + 1) * HEAD_DIM, 0:n_keys]
            outs.append(_dot(vt, p) / l)
        o_ref[...] = jnp.concatenate(outs, axis=0).T

    for blk in range(seq // tq):
        pl.when(i == blk)(functools.partial(attend, (blk + 1) * tq))


def _fox_prompt(fq, fkt, fvt, layer, cum, cumt, bsz, seq, tq, pairs):
    nq = seq // tq
    width = pairs * LANES
    groups = FOX_W // width
    prep_chunk = _pick_tile(seq, 512)
    kv = pl.BlockSpec((None, None, width, seq), lambda b, g, i: (layer, b, g, 0))
    return pl.pallas_call(
        functools.partial(_fox_prompt_kernel, tq=tq, prep_chunk=prep_chunk, pairs=pairs),
        grid=(bsz, groups, nq),
        in_specs=[pl.BlockSpec((tq, width), lambda b, g, i: (b * nq + i, g)),
                  kv, kv,
                  pl.BlockSpec((seq, LANES), lambda b, g, i: (b, 0)),
                  pl.BlockSpec((None, H_FOX, seq), lambda b, g, i: (b, 0, 0))],
        out_specs=pl.BlockSpec((tq, width), lambda b, g, i: (b * nq + i, g)),
        out_shape=jax.ShapeDtypeStruct((bsz * seq, FOX_W), F32),
        scratch_shapes=[pltpu.VMEM((pairs, seq, 2 * LANES), BF16), pltpu.VMEM((width, seq), BF16)],
        compiler_params=pltpu.CompilerParams(
            dimension_semantics=("arbitrary", "arbitrary", "arbitrary"),
            vmem_limit_bytes=VMEM_LIMIT),
        name="fox_prompt",
    )(fq, fkt, fvt, cum, cumt)


def _fox_sample_kernel(pt_ref, q_ref, kn_ref, vn_ref, lfn_ref, w_ref, *rest, pages_per_step):
    del pt_ref
    g = pages_per_step
    k_refs = rest[:g]
    v_refs = rest[g:2 * g]
    lf_refs = rest[2 * g:3 * g]
    o_ref = rest[3 * g]
    m_sc, l_sc, acc_sc, after_sc = rest[3 * g + 1:]
    jg = pl.program_id(1)

    sub = lax.broadcasted_iota(jnp.int32, (H_FOX, FOX_W), 0)
    lane = lax.broadcasted_iota(jnp.int32, (H_FOX, FOX_W), 1)
    hmask = (lane // HEAD_DIM) == sub
    qrows = jnp.where(hmask, q_ref[...] * ATTN_SCALE, 0.0)

    @pl.when(jg == 0)
    def _():
        m_sc[...] = jnp.full_like(m_sc, NEG_BIG)
        l_sc[...] = jnp.zeros_like(l_sc)
        acc_sc[...] = jnp.zeros_like(acc_sc)
        after_sc[...] = jnp.zeros_like(after_sc)

    qb = qrows.astype(BF16)
    w = w_ref[...]
    m = m_sc[...]
    l = l_sc[...]
    after = after_sc[...]
    lf_all = jnp.concatenate([r[...] for r in lf_refs], axis=0)
    a, b, c = _split3(lf_all)
    st_all = (_dot(a, w) + _dot(b, w)) + _dot(c, w)
    scores = []
    for t in range(g):
        st = st_all[t * H_FOX:(t + 1) * H_FOX]
        kb = k_refs[t][...].astype(BF16)
        scores.append(_dot(qb, kb) + st[:, :PAGE_SIZE] + after)
        after = after + st[:, PAGE_SIZE:]
    s_max = scores[0]
    for t in range(1, g):
        s_max = jnp.maximum(s_max, scores[t])
    m_new = jnp.maximum(m, jnp.max(s_max, axis=-1, keepdims=True))
    alpha = jnp.exp(m - m_new)
    p_sum = None
    pv = None
    for t in range(g):
        p = jnp.exp(scores[t] - m_new)
        p_sum = p if p_sum is None else p_sum + p
        d = _dot_nt(p.astype(BF16), v_refs[t][...].astype(BF16))
        pv = d if pv is None else pv + d
    l = alpha * l + jnp.sum(p_sum, axis=-1, keepdims=True)
    acc = alpha * acc_sc[...] + pv
    m = m_new
    m_sc[...] = m
    l_sc[...] = l
    acc_sc[...] = acc
    after_sc[...] = after

    @pl.when(jg == pl.num_programs(1) - 1)
    def _():
        s_new = jnp.sum(qrows * kn_ref[...], axis=-1, keepdims=True) - lfn_ref[...]
        m_fin = jnp.maximum(m, s_new)
        alpha = jnp.exp(m - m_fin)
        p_new = jnp.exp(s_new - m_fin)
        l_fin = alpha * l + p_new
        acc_fin = alpha * acc + p_new * vn_ref[...]
        out = jnp.where(hmask, acc_fin / l_fin, 0.0)
        o_ref[...] = jnp.sum(out, axis=0, keepdims=True)


def _fox_sample(fq, fk, fv, lf, ck_t, cv_t, clf_t, page_table, layer, w_sfx, g):
    bsz = fq.shape[0]
    n_pages = page_table.shape[1]
    n_steps = n_pages // g
    q3 = fq.reshape(bsz, 1, FOX_W)
    k3 = fk.reshape(bsz, 1, FOX_W)
    v3 = fv.reshape(bsz, 1, FOX_W)
    lf3 = lf.reshape(bsz, H_FOX, 1)
    row = pl.BlockSpec((None, 1, FOX_W), lambda b, j, pt: (b, 0, 0))

    def page_spec(rows, t):
        def imap(b, j, pt):
            return (layer, pt[b, n_pages - 1 - (j * g + t)], 0, 0)
        return pl.BlockSpec((None, None, rows, PAGE_SIZE), imap)

    in_specs = [row, row, row,
                pl.BlockSpec((None, H_FOX, 1), lambda b, j, pt: (b, 0, 0)),
                pl.BlockSpec(w_sfx.shape, lambda b, j, pt: (0, 0))]
    in_specs += [page_spec(FOX_W, t) for t in range(g)]
    in_specs += [page_spec(FOX_W, t) for t in range(g)]
    in_specs += [page_spec(H_FOX, t) for t in range(g)]
    out = pl.pallas_call(
        functools.partial(_fox_sample_kernel, pages_per_step=g),
        grid_spec=pltpu.PrefetchScalarGridSpec(
            num_scalar_prefetch=1,
            grid=(bsz, n_steps),
            in_specs=in_specs,
            out_specs=row,
            scratch_shapes=[pltpu.VMEM((H_FOX, 1), F32), pltpu.VMEM((H_FOX, 1), F32),
                            pltpu.VMEM((H_FOX, FOX_W), F32), pltpu.VMEM((H_FOX, PAGE_SIZE), F32)]),
        out_shape=jax.ShapeDtypeStruct((bsz, 1, FOX_W), F32),
        compiler_params=pltpu.CompilerParams(
            dimension_semantics=("arbitrary", "arbitrary"), vmem_limit_bytes=VMEM_LIMIT),
        name="fox_sample",
    )(page_table, q3, k3, v3, lf3, w_sfx, *([ck_t] * g), *([cv_t] * g), *([clf_t] * g))
    return out.reshape(bsz, FOX_W)


def _merge_out(x_ref, ro_ref, rg_ref, fo_ref, y_c, seg_ref, retg_ref, wout_ref):
    y_ret = jax.nn.silu(rg_ref[...]) * _head_rms(ro_ref[...], seg_ref[...], retg_ref[...])
    y = jnp.concatenate([y_ret, fo_ref[...], y_c], axis=-1).astype(BF16)
    return x_ref[...] + _dot(y, wout_ref[...])


def _merge_prompt(x_ref, ro_ref, rg_ref, fo_ref, cu_ref, cv_ref, seg_ref, retg_ref,
                  sw_ref, sb_ref, wout_ref):
    tm = x_ref.shape[0]
    r = lax.broadcasted_iota(jnp.int32, (CHUNK, CHUNK), 0)
    c = lax.broadcasted_iota(jnp.int32, (CHUNK, CHUNK), 1)
    lane = lax.broadcasted_iota(jnp.int32, (CHUNK, CMLP_W), 1)
    ws = [jnp.where(r >= c, sw_ref[g], 0.0).astype(BF16) for g in range(C_GROUPS)]
    sb = sb_ref[...]
    chunks = []
    for n in range(tm // CHUNK):
        vb = cv_ref[n * CHUNK:(n + 1) * CHUNK, :].astype(BF16)
        mixed = sb
        for g in range(C_GROUPS):
            mixed = mixed + jnp.where((lane // HEAD_DIM) == g, _dot(ws[g], vb), 0.0)
        chunks.append(cu_ref[n * CHUNK:(n + 1) * CHUNK, :] * mixed)
    y_c = chunks[0] if len(chunks) == 1 else jnp.concatenate(chunks, axis=0)
    return _merge_out(x_ref, ro_ref, rg_ref, fo_ref, y_c, seg_ref, retg_ref, wout_ref)


def _merge_sample_kernel(x_ref, ro_ref, rg_ref, fo_ref, cu_ref, cv_ref, seg_ref, retg_ref,
                         sw0_ref, sb0_ref, wout_ref, out_ref):
    y_c = cu_ref[...] * (sw0_ref[...] * cv_ref[...] + sb0_ref[...])
    out_ref[...] = _merge_out(x_ref, ro_ref, rg_ref, fo_ref, y_c, seg_ref, retg_ref, wout_ref)


def _merge_specs(tm, d, wout, sample, row, const):
    if sample:
        sw_spec = pl.BlockSpec((1, CMLP_W), const)
        sb_spec = pl.BlockSpec((1, CMLP_W), const)
    else:
        sw_spec = pl.BlockSpec((C_GROUPS, CHUNK, CHUNK), lambda i: (0, 0, 0))
        sb_spec = pl.BlockSpec((CHUNK, CMLP_W), const)
    return [pl.BlockSpec((tm, d), row),
            pl.BlockSpec((tm, RET_W), row), pl.BlockSpec((tm, RET_W), row),
            pl.BlockSpec((tm, FOX_W), row),
            pl.BlockSpec((tm, CMLP_W), row), pl.BlockSpec((tm, CMLP_W), row),
            pl.BlockSpec((256, 256), const), pl.BlockSpec((1, RET_W), const),
            sw_spec, sb_spec,
            pl.BlockSpec(wout.shape, const)]


def _merge_sample(x2d, ro, rg, fo, cu, cv, seg, retg, sw0, sb0, wout):
    n, d = x2d.shape
    row = lambda i: (i, 0)
    const = lambda i: (0, 0)
    return pl.pallas_call(
        _merge_sample_kernel,
        grid=(1,),
        in_specs=_merge_specs(n, d, wout, True, row, const),
        out_specs=pl.BlockSpec((n, d), row),
        out_shape=jax.ShapeDtypeStruct((n, d), F32),
        compiler_params=pltpu.CompilerParams(
            dimension_semantics=("arbitrary",), vmem_limit_bytes=VMEM_LIMIT),
        name="merge_sample",
    )(x2d, ro, rg, fo, cu, cv, seg, retg, sw0, sb0, wout)


def _ffn_prompt_kernel(*refs, tiles_per_seq, fb):
    merge_refs = refs[:11]
    g2_ref, wu_ref, cw_ref, cb_ref, wd_ref, prev_ref, out_ref, st_ref, act_sc, carry_sc = refs[11:]
    t = pl.program_id(0) % tiles_per_seq
    tm = out_ref.shape[0]
    d_ff = wd_ref.shape[0]

    @pl.when(t == 0)
    def _():
        carry_sc[...] = prev_ref[...]

    x = _merge_prompt(*merge_refs)
    hb = _row_rms(x, g2_ref[...]).astype(BF16)
    row = lax.broadcasted_iota(jnp.int32, (tm, fb), 0)

    def conv_cols(lo):
        u = _dot(hb, wu_ref[:, lo:lo + fb])
        prev = carry_sc[:, lo:lo + fb]
        s1 = jnp.where(row == 0, prev[1:2], pltpu.roll(u, 1, 0))
        s2 = jnp.where(row == 0, prev[0:1], jnp.where(row == 1, prev[1:2], pltpu.roll(u, 2, 0)))
        carry_sc[:, lo:lo + fb] = u[tm - (CONV_W - 1):tm, :]
        return (((cb_ref[:, lo:lo + fb] + cw_ref[0:1, lo:lo + fb] * s2)
                 + cw_ref[1:2, lo:lo + fb] * s1) + cw_ref[2:3, lo:lo + fb] * u)

    for c in range(d_ff // fb):
        gate = conv_cols(c * fb)
        val = conv_cols(d_ff + c * fb)
        act_sc[:, c * fb:(c + 1) * fb] = (jax.nn.silu(gate) * val).astype(BF16)
    st_ref[...] = carry_sc[...]
    out_ref[...] = x + _dot(act_sc[...], wd_ref[...])


def _ffn_sample_kernel(x_ref, g2_ref, wug_ref, wuv_ref, cwg_ref, cwv_ref, cbg_ref, cbv_ref, wd_ref,
                       p0g_ref, p0v_ref, p1g_ref, p1v_ref, out_ref, st_ref, h_sc, acc_sc, *, d_ff):
    j = pl.program_id(1)
    fb = wug_ref.shape[1]

    @pl.when(j == 0)
    def _():
        h_sc[...] = _row_rms(x_ref[...], g2_ref[...]).astype(BF16)
        acc_sc[...] = jnp.zeros_like(acc_sc)

    hb = h_sc[...]

    def conv(u, p0, p1, w_ref, b_ref):
        return ((b_ref[...] + w_ref[0:1] * p0) + w_ref[1:2] * p1) + w_ref[2:3] * u

    ug = _dot(hb, wug_ref[...])
    uv = _dot(hb, wuv_ref[...])
    p1g = p1g_ref[...]
    p1v = p1v_ref[...]
    conv_g = conv(ug, p0g_ref[...], p1g, cwg_ref, cbg_ref)
    conv_v = conv(uv, p0v_ref[...], p1v, cwv_ref, cbv_ref)
    cg = pl.multiple_of(j * fb, LANES)
    cv = pl.multiple_of(d_ff + j * fb, LANES)
    st_ref[:, pl.ds(cg, fb)] = p1g
    st_ref[:, pl.ds(cv, fb)] = p1v
    st_ref[:, pl.ds(pl.multiple_of(2 * d_ff + j * fb, LANES), fb)] = ug
    st_ref[:, pl.ds(pl.multiple_of(3 * d_ff + j * fb, LANES), fb)] = uv
    act = (jax.nn.silu(conv_g) * conv_v).astype(BF16)
    acc_sc[...] += _dot(act, wd_ref[...])

    @pl.when(j == pl.num_programs(1) - 1)
    def _():
        out_ref[...] = x_ref[...] + acc_sc[...]


def _ffn(x2d, g2, wup, cw, cb, wdown, prev, tm, fb, seq, merge_args=None):
    n, d = x2d.shape
    d_ff = wdown.shape[0]
    nff = d_ff // fb
    nt = n // tm
    xspec = pl.BlockSpec((tm, d), lambda i, j: (i, 0))
    common = [xspec,
              pl.BlockSpec((1, d), lambda i, j: (0, 0)),
              pl.BlockSpec((d, fb), lambda i, j: (0, j)),
              pl.BlockSpec((d, fb), lambda i, j: (0, nff + j)),
              pl.BlockSpec((CONV_W, fb), lambda i, j: (0, j)),
              pl.BlockSpec((CONV_W, fb), lambda i, j: (0, nff + j)),
              pl.BlockSpec((1, fb), lambda i, j: (0, j)),
              pl.BlockSpec((1, fb), lambda i, j: (0, nff + j)),
              pl.BlockSpec((fb, d), lambda i, j: (j, 0))]
    scratch = [pltpu.VMEM((tm, d), BF16), pltpu.VMEM((tm, d), F32)]
    params = pltpu.CompilerParams(
        dimension_semantics=("arbitrary", "arbitrary"), vmem_limit_bytes=VMEM_LIMIT)
    if seq == 1:
        bsz = n
        prev2 = prev.reshape(bsz, (CONV_W - 1) * 2 * d_ff)
        pspecs = [pl.BlockSpec((bsz, fb), lambda i, j, o=o: (0, o + j))
                  for o in (0, nff, 2 * nff, 3 * nff)]
        out, st = pl.pallas_call(
            functools.partial(_ffn_sample_kernel, d_ff=d_ff),
            grid=(nt, nff),
            in_specs=common + [pspecs[0], pspecs[1], pspecs[2], pspecs[3]],
            out_specs=[xspec, pl.BlockSpec(prev2.shape, lambda i, j: (0, 0))],
            out_shape=[jax.ShapeDtypeStruct((n, d), F32), jax.ShapeDtypeStruct(prev2.shape, F32)],
            scratch_shapes=scratch,
            compiler_params=params,
            name="ffn_sample",
        )(x2d, g2, wup, wup, cw, cw, cb, cb, wdown, prev2, prev2, prev2, prev2)
        return out, st.reshape(prev.shape)
    tiles_per_seq = seq // tm
    const = lambda i: (0, 0)
    resident = functools.partial(pl.BlockSpec, pipeline_mode=pl.Buffered(1))
    state = pl.BlockSpec((None, CONV_W - 1, 2 * d_ff), lambda i: (i // tiles_per_seq, 0, 0))
    row = pl.BlockSpec((tm, d), lambda i: (i, 0))
    wout = merge_args[-1]
    merge_specs = _merge_specs(tm, d, wout, False, lambda i: (i, 0), const)
    merge_specs[-1] = resident(wout.shape, const)
    return pl.pallas_call(
        functools.partial(_ffn_prompt_kernel, tiles_per_seq=tiles_per_seq, fb=fb),
        grid=(nt,),
        in_specs=merge_specs + [
            pl.BlockSpec((1, d), const),
            resident(wup.shape, const),
            pl.BlockSpec(cw.shape, const),
            pl.BlockSpec(cb.shape, const),
            resident(wdown.shape, const),
            state],
        out_specs=[row, state],
        out_shape=[jax.ShapeDtypeStruct((n, d), F32), jax.ShapeDtypeStruct(prev.shape, F32)],
        scratch_shapes=[pltpu.VMEM((tm, d_ff), BF16), pltpu.VMEM((CONV_W - 1, 2 * d_ff), F32)],
        compiler_params=pltpu.CompilerParams(
            dimension_semantics=("arbitrary",), vmem_limit_bytes=VMEM_LIMIT),
        name="ffn_prompt",
    )(x2d, *merge_args, g2, wup, cw, cb, wdown, prev)


def _rope_tables(pos):
    half = HEAD_DIM // 2
    inv = 1.0 / (ROPE_BASE ** (jnp.arange(half, dtype=F32) / half))
    ang = pos.astype(F32)[:, None] * inv[None, :]
    cos = jnp.cos(ang)
    sin = jnp.sin(ang)
    cos_e = jnp.tile(jnp.concatenate([cos, cos], axis=-1), (1, H_RET))
    sin_e = jnp.tile(jnp.concatenate([-sin, sin], axis=-1), (1, H_RET))
    return cos_e, sin_e


def _retention_tables():
    c = CHUNK
    lg = jnp.log1p(-jnp.exp2(-5.0 - jnp.arange(H_RET, dtype=F32)))
    idx = jnp.arange(c, dtype=F32)
    diff = idx[:, None] - idx[None, :]
    intra = jnp.where(diff[None] >= 0, jnp.exp(jnp.maximum(diff, 0.0)[None] * lg[:, None, None]), 0.0)
    q_dec = jnp.exp((idx[None, :] + 1.0) * lg[:, None])
    k_dec = jnp.exp((c - 1.0 - idx[None, :]) * lg[:, None])
    chunk_dec = jnp.exp(c * lg)
    expand = lambda a: jnp.repeat(a.T, HEAD_DIM, axis=1)
    head_of = jnp.arange(RET_W) // HEAD_DIM
    bd = (head_of[:, None] == head_of[None, :]).astype(F32)
    decmat = jnp.broadcast_to(chunk_dec[head_of][:, None], (RET_W, RET_W))
    gamma = jnp.exp(1.0 * lg)
    gcol = jnp.broadcast_to(gamma[:, None, None], (H_RET, HEAD_DIM, 1))
    grow = jnp.broadcast_to(gamma[:, None, None], (H_RET, 1, HEAD_DIM))
    return dict(intra=intra, qdec=expand(q_dec), kdec=expand(k_dec), decmat=decmat, bd=bd,
                gcol=gcol, grow=grow)


def _pick_tile(n, target):
    t = min(n, target)
    while n % t:
        t //= 2
    return t


@jax.jit
def kernel(x_prompt, x_sample, cache_k, cache_v, cache_logf, state_ret, state_ffn_conv, page_table,
           norm1_g, w_in, b_forget, ret_norm_g, q_norm_g, k_norm_g, spatial_w, spatial_b, w_out,
           norm2_g, w_up, conv_w, conv_b, w_down):
    bp, lp, d = x_prompt.shape
    bs, ls, _ = x_sample.shape
    depth = w_in.shape[0]
    d_ff = w_down.shape[1]
    n_pages = page_table.shape[1]
    past_len = n_pages * PAGE_SIZE
    assert ls == 1 and lp % CHUNK == 0

    fz0 = 4 * RET_W + 3 * FOX_W
    w_t = jnp.swapaxes(w_in, 1, 2)
    w_in_p = jnp.concatenate(
        [w_t[:, :fz0], w_t[:, fz0 + H_FOX:], w_t[:, fz0:fz0 + H_FOX],
         jnp.zeros((depth, FZ_COLS - H_FOX, d), w_in.dtype)], axis=1).astype(BF16)
    w_out_b = w_out.astype(BF16)
    w_up_b = w_up.astype(BF16)
    w_down_b = w_down.astype(BF16)
    bf_p = jnp.pad(b_forget, ((0, 0), (0, FZ_COLS - H_FOX)))
    qg_e = jnp.tile(q_norm_g, (1, H_FOX))
    kg_e = jnp.tile(k_norm_g, (1, H_FOX))
    retg_e = ret_norm_g.reshape(depth, RET_W)
    sb_e = jnp.repeat(jnp.swapaxes(spatial_b, 1, 2), HEAD_DIM, axis=2)
    sw0_e = jnp.repeat(spatial_w[:, :, 0, 0], HEAD_DIM, axis=1)
    sb0_e = jnp.repeat(spatial_b[:, :, 0], HEAD_DIM, axis=1)

    head_of = jnp.arange(256) // HEAD_DIM
    seg = (head_of[:, None] == head_of[None, :]).astype(BF16)
    idx = jnp.arange(CHUNK)
    tri_incl = (idx[:, None] >= idx[None, :]).astype(BF16)
    sfx = (idx[:, None] > idx[None, :]).astype(BF16)
    w_sfx = jnp.concatenate([sfx, jnp.ones((PAGE_SIZE, PAGE_SIZE), BF16)], axis=1)
    ck_t = jnp.transpose(cache_k, (0, 1, 3, 4, 2)).reshape(depth, -1, FOX_W, PAGE_SIZE)
    cv_t = jnp.transpose(cache_v, (0, 1, 3, 4, 2)).reshape(depth, -1, FOX_W, PAGE_SIZE)
    clf_t = jnp.transpose(cache_logf, (0, 1, 3, 2))
    cos_p, sin_p = _rope_tables(jnp.arange(lp))
    cos_s, sin_s = _rope_tables(jnp.broadcast_to(past_len + jnp.arange(ls), (bs,)))
    tabs = _retention_tables()

    tm_p = _pick_tile(lp, 512)
    tq = _pick_tile(lp, 256)
    tm_f = _pick_tile(lp, 512)
    fb = 256
    fb_s = d_ff // 2 if (d_ff // 2) % LANES == 0 else fb
    g_pages = _pick_tile(n_pages, 32)
    zeros_prev = jnp.zeros((bp, CONV_W - 1, 2 * d_ff), F32)

    xp = x_prompt.reshape(bp * lp, d)
    xs = x_sample.reshape(bs * ls, d)
    outs = [[] for _ in range(11)]
    fkt = fvt = None
    for l in range(depth):
        g1 = norm1_g[l][None]
        g2 = norm2_g[l][None]
        bfl = bf_p[l][None]
        qg = qg_e[l][None]
        kg = kg_e[l][None]
        retg = retg_e[l][None]
        cb = conv_b[l][None]

        rq, rk, rv, rg, fq, fkt, fvt, _, lfp, cu, cv = _proj(
            xp, cos_p, sin_p, g1, w_in_p[l], bfl, qg, kg, seg, tm_p, (depth, l, fkt, fvt))
        ro, r_state = _ret_prompt(rq, rk, rv, tabs, bp, lp)
        cum, cumt, lft = _cum(lfp, tri_incl, bp, lp)
        fo = _fox_prompt(fq, fkt, fvt, l, cum, cumt, bp, lp, tq, 2)
        xp, c_state = _ffn(xp, g2, w_up_b[l], conv_w[l], cb, w_down_b[l], zeros_prev, tm_f, fb, lp,
                           (ro, rg, fo, cu, cv, seg, retg, spatial_w[l], sb_e[l], w_out_b[l]))
        outs[2].append(lft)
        outs[3].append(r_state)
        outs[4].append(c_state)

        rq, rk, rv, rg, fq, fk, fv, lf, lfp, cu, cv = _proj(
            xs, cos_s, sin_s, g1, w_in_p[l], bfl, qg, kg, seg, bs)
        ro, r_state = _ret_sample(rq, rk, rv, state_ret, l, tabs)
        fo = _fox_sample(fq, fk, fv, lf, ck_t, cv_t, clf_t, page_table, l, w_sfx, g_pages)
        xs = _merge_sample(xs, ro, rg, fo, cu, cv, seg, retg, sw0_e[l][None], sb0_e[l][None],
                           w_out_b[l])
        xs, c_state = _ffn(xs, g2, w_up_b[l], conv_w[l], cb, w_down_b[l], state_ffn_conv[l], bs, fb_s, 1)
        outs[5].append(fk.reshape(bs, ls, H_FOX, HEAD_DIM))
        outs[6].append(fv.reshape(bs, ls, H_FOX, HEAD_DIM))
        outs[7].append(lf.reshape(bs, ls, H_FOX))
        outs[8].append(r_state)
        outs[9].append(cv.reshape(bs, ls, CMLP_W))
        outs[10].append(c_state)

    stacked = [jnp.stack(o) if o else None for o in outs]
    for o, buf in ((0, fkt), (1, fvt)):
        stacked[o] = jnp.transpose(buf.reshape(depth, bp, H_FOX, HEAD_DIM, lp), (0, 1, 4, 2, 3))
    stacked[2] = jnp.transpose(stacked[2], (0, 1, 3, 2))
    return (xp.reshape(bp, lp, d), xs.reshape(bs, ls, d), *stacked)
```

```python
import functools

import jax
import jax.numpy as jnp
from jax import lax
from jax.experimental import pallas as pl
from jax.experimental.pallas import tpu as pltpu

HEAD_DIM = 64
H_RET = 4
H_FOX = 8
C_GROUPS = 4
RET_W = H_RET * HEAD_DIM
FOX_W = H_FOX * HEAD_DIM
CMLP_W = C_GROUPS * HEAD_DIM
CHUNK = 128
PAGE_SIZE = 128
CONV_W = 3
RMS_EPS = 1e-6
ROPE_BASE = 10000.0
ATTN_SCALE = HEAD_DIM ** -0.5

LANES = 128
FZ_COLS = LANES
VMEM_LIMIT = 56 * 1024 * 1024
NEG_BIG = -1e30
LOG2E = 1.4426950408889634

F32 = jnp.float32
BF16 = jnp.bfloat16


def _dot(a, b):
    return jnp.dot(a, b, preferred_element_type=F32)


def _dot_nt(a, b):
    return lax.dot_general(a, b, (((1,), (1,)), ((), ())), preferred_element_type=F32)


def _split2(x):
    hi = x.astype(BF16)
    lo = (x - hi.astype(F32)).astype(BF16)
    return hi, lo


def _split3(x):
    a = x.astype(BF16)
    r = x - a.astype(F32)
    b = r.astype(BF16)
    c = (r - b.astype(F32)).astype(BF16)
    return a, b, c


def _seg_mean(sq, seg):
    hi, lo = _split2(sq)
    return (_dot(hi, seg) + _dot(lo, seg)) * (1.0 / HEAD_DIM)


def _head_rms(x, seg, g):
    parts = []
    for c in range(x.shape[-1] // 256):
        xc = x[:, c * 256:(c + 1) * 256]
        parts.append(xc * lax.rsqrt(_seg_mean(xc * xc, seg) + RMS_EPS))
    y = parts[0] if len(parts) == 1 else jnp.concatenate(parts, axis=-1)
    return y * g


def _layer_spec(stacked, block, index_map, **kwargs):
    _, layer = stacked
    return pl.BlockSpec((None,) + tuple(block), lambda *a: (layer,) + tuple(index_map(*a)), **kwargs)


def _row_rms(x, g):
    return x * lax.rsqrt(jnp.mean(x * x, axis=-1, keepdims=True) + RMS_EPS) * g


def _proj_kernel(x_ref, cos_ref, sin_ref, g1_ref, w_ref, bf_ref, qg_ref, kg_ref, seg_ref, *refs,
                 kv_transposed):
    (rq_ref, rk_ref, rv_ref, rg_ref, fq_ref, fk_ref, fv_ref, lf_ref, lfp_ref,
     cu_ref, cv_ref) = refs[-11:]
    hb = _row_rms(x_ref[...], g1_ref[...]).astype(BF16)
    seg = seg_ref[...]

    def proj(lo, n):
        return _dot_nt(hb, w_ref[lo:lo + n, :])

    cos = cos_ref[...]
    sin = sin_ref[...]
    lane = lax.broadcasted_iota(jnp.int32, cos.shape, 1)
    first_half = (lane % HEAD_DIM) < (HEAD_DIM // 2)

    def rot(z):
        swapped = jnp.where(first_half, pltpu.roll(z, RET_W - HEAD_DIM // 2, 1),
                            pltpu.roll(z, HEAD_DIM // 2, 1))
        return z * cos + swapped * sin

    rq_ref[...] = rot(proj(0, RET_W))
    rk_ref[...] = rot(proj(RET_W, RET_W)) * ATTN_SCALE
    rv_ref[...] = proj(2 * RET_W, RET_W)
    rg_ref[...] = proj(3 * RET_W, RET_W)
    off = 4 * RET_W
    fq_ref[...] = _head_rms(proj(off, FOX_W), seg, qg_ref[...])
    fk = _head_rms(proj(off + FOX_W, FOX_W), seg, kg_ref[...])
    fv = proj(off + 2 * FOX_W, FOX_W)
    fk_ref[...] = fk.T if kv_transposed else fk
    fv_ref[...] = fv.T if kv_transposed else fv
    off += 3 * FOX_W
    cu_ref[...] = jax.nn.gelu(proj(off, CMLP_W))
    cv_ref[...] = jax.nn.gelu(proj(off + CMLP_W, CMLP_W))
    off += 2 * CMLP_W
    lf = jax.nn.log_sigmoid(proj(off, FZ_COLS) + bf_ref[...])
    lane_z = lax.broadcasted_iota(jnp.int32, lf.shape, 1)
    lfp_ref[...] = jnp.where(lane_z < H_FOX, lf, 0.0)
    lf_ref[...] = lf[:, :H_FOX]


def _proj(x2d, cos, sin, g1, w, bfp, qg, kg, seg, tm, kv_step=None):
    kv_transposed = kv_step is not None
    n, d = x2d.shape
    nt = n // tm
    tt = cos.shape[0] // tm
    row = lambda i: (i, 0)
    const = lambda i: (0, 0)
    outs = [RET_W] * 4 + [FOX_W] * 3 + [H_FOX, FZ_COLS] + [CMLP_W] * 2
    out_specs = [pl.BlockSpec((tm, c), row) for c in outs]
    out_shape = [jax.ShapeDtypeStruct((n, c), F32) for c in outs]
    in_specs = [
        pl.BlockSpec((tm, d), row),
        pl.BlockSpec((tm, RET_W), lambda i: (i % tt, 0)),
        pl.BlockSpec((tm, RET_W), lambda i: (i % tt, 0)),
        pl.BlockSpec((1, d), const),
        _layer_spec(w, w[0].shape[1:], const),
        pl.BlockSpec((1, FZ_COLS), const),
        pl.BlockSpec((1, FOX_W), const),
        pl.BlockSpec((1, FOX_W), const),
        pl.BlockSpec((256, 256), const),
    ]
    args = [x2d, cos, sin, g1, w[0], bfp, qg, kg, seg]
    aliases = {}
    if kv_transposed:
        depth, layer, k_buf, v_buf = kv_step
        for o in (5, 6):
            out_specs[o] = pl.BlockSpec((None, None, FOX_W, tm),
                                        lambda i: (layer, i // tt, 0, i % tt))
            out_shape[o] = jax.ShapeDtypeStruct((depth, nt // tt, FOX_W, tt * tm), F32)
        if k_buf is not None:
            aliases = {len(args): 5, len(args) + 1: 6}
            in_specs += [pl.BlockSpec(memory_space=pl.ANY)] * 2
            args += [k_buf, v_buf]
    return pl.pallas_call(
        functools.partial(_proj_kernel, kv_transposed=kv_transposed),
        grid=(nt,),
        in_specs=in_specs,
        out_specs=out_specs,
        out_shape=out_shape,
        input_output_aliases=aliases,
        compiler_params=pltpu.CompilerParams(
            dimension_semantics=("arbitrary",), vmem_limit_bytes=VMEM_LIMIT),
        name="proj",
    )(*args)


def _ret_prompt_kernel(q_ref, k_ref, v_ref, intra_ref, qdec_ref, kdec_ref, dec_ref, bd_ref,
                       o_ref, st_ref, s_sc, *, chunks):
    c = pl.program_id(1)

    @pl.when(c == 0)
    def _():
        s_sc[...] = jnp.zeros_like(s_sc)

    lane = lax.broadcasted_iota(jnp.int32, (CHUNK, RET_W), 1)
    state = s_sc[...]
    for n in range(chunks):
        sl = slice(n * CHUNK, (n + 1) * CHUNK)
        q = q_ref[sl, :]
        k = k_ref[sl, :]
        vb = v_ref[sl, :].astype(BF16)
        kb = k.astype(BF16)
        o = _dot((q * qdec_ref[...]).astype(BF16), state.astype(BF16))
        for h in range(H_RET):
            hm = (lane // HEAD_DIM) == h
            sc = _dot_nt(jnp.where(hm, q, 0.0).astype(BF16), kb) * intra_ref[h]
            o = o + jnp.where(hm, _dot(sc.astype(BF16), vb), 0.0)
        o_ref[sl, :] = o
        kd = (k * kdec_ref[...]).astype(BF16)
        ktv = lax.dot_general(kd, vb, (((0,), (0,)), ((), ())), preferred_element_type=F32)
        state = state * dec_ref[...] + bd_ref[...] * ktv
    s_sc[...] = state

    @pl.when(c == pl.num_programs(1) - 1)
    def _():
        for h in range(H_RET):
            lo = h * HEAD_DIM
            st_ref[h] = state[lo:lo + HEAD_DIM, lo:lo + HEAD_DIM]


def _ret_prompt(rq, rk, rv, tabs, bsz, seq):
    chunks = _pick_tile(seq // CHUNK, 4)
    rows = chunks * CHUNK
    ns = seq // rows
    blk = pl.BlockSpec((rows, RET_W), lambda b, c: (b * ns + c, 0))
    c2 = lambda b, c: (0, 0)
    return pl.pallas_call(
        functools.partial(_ret_prompt_kernel, chunks=chunks),
        grid=(bsz, ns),
        in_specs=[blk, blk, blk,
                  pl.BlockSpec((H_RET, CHUNK, CHUNK), lambda b, c: (0, 0, 0)),
                  pl.BlockSpec((CHUNK, RET_W), c2),
                  pl.BlockSpec((CHUNK, RET_W), c2),
                  pl.BlockSpec((RET_W, RET_W), c2),
                  pl.BlockSpec((RET_W, RET_W), c2)],
        out_specs=[blk, pl.BlockSpec((None, H_RET, HEAD_DIM, HEAD_DIM), lambda b, c: (b, 0, 0, 0))],
        out_shape=[jax.ShapeDtypeStruct((bsz * seq, RET_W), F32),
                   jax.ShapeDtypeStruct((bsz, H_RET, HEAD_DIM, HEAD_DIM), F32)],
        scratch_shapes=[pltpu.VMEM((RET_W, RET_W), F32)],
        compiler_params=pltpu.CompilerParams(
            dimension_semantics=("arbitrary", "arbitrary"), vmem_limit_bytes=VMEM_LIMIT),
        name="ret_prompt",
    )(rq, rk, rv, tabs["intra"], tabs["qdec"], tabs["kdec"], tabs["decmat"], tabs["bd"])


def _ret_sample_kernel(q_ref, k_ref, v_ref, s_ref, gcol_ref, grow_ref, o_ref, st_ref):
    for b in range(q_ref.shape[0]):
        for h in range(H_RET):
            qc = q_ref[b, h]
            kc = k_ref[b, h]
            vr = v_ref[b, h]
            s = s_ref[b, h]
            qk = jnp.sum(qc * kc, axis=0, keepdims=True)
            inter = jnp.sum((qc * gcol_ref[h]) * s, axis=0, keepdims=True)
            o_ref[b, h] = qk * vr + inter
            st_ref[b, h] = s * grow_ref[h] + kc * vr


def _ret_sample(rq, rk, rv, state, layer, tabs):
    bsz = rq.shape[0]
    nb = _pick_tile(bsz, 8)
    qc = rq.reshape(bsz, H_RET, HEAD_DIM, 1)
    kc = rk.reshape(bsz, H_RET, HEAD_DIM, 1)
    vr = rv.reshape(bsz, H_RET, 1, HEAD_DIM)
    col = pl.BlockSpec((nb, H_RET, HEAD_DIM, 1), lambda b: (b, 0, 0, 0))
    rowb = pl.BlockSpec((nb, H_RET, 1, HEAD_DIM), lambda b: (b, 0, 0, 0))
    stb = pl.BlockSpec((nb, H_RET, HEAD_DIM, HEAD_DIM), lambda b: (b, 0, 0, 0))
    o, st = pl.pallas_call(
        _ret_sample_kernel,
        grid=(bsz // nb,),
        in_specs=[col, col, rowb,
                  pl.BlockSpec((None, nb, H_RET, HEAD_DIM, HEAD_DIM), lambda b: (layer, b, 0, 0, 0)),
                  pl.BlockSpec((H_RET, HEAD_DIM, 1), lambda b: (0, 0, 0)),
                  pl.BlockSpec((H_RET, 1, HEAD_DIM), lambda b: (0, 0, 0))],
        out_specs=[rowb, stb],
        out_shape=[jax.ShapeDtypeStruct((bsz, H_RET, 1, HEAD_DIM), F32),
                   jax.ShapeDtypeStruct((bsz, H_RET, HEAD_DIM, HEAD_DIM), F32)],
        compiler_params=pltpu.CompilerParams(dimension_semantics=("arbitrary",)),
        name="ret_sample",
    )(qc, kc, vr, state, tabs["gcol"], tabs["grow"])
    return o.reshape(bsz, RET_W), st


def _cum_kernel(lf_ref, tri_ref, cum_ref, cumt_ref, lft_ref):
    tri = tri_ref[...]
    carry = jnp.zeros((1, LANES), F32)
    for c in range(lf_ref.shape[0] // CHUNK):
        sl = slice(c * CHUNK, (c + 1) * CHUNK)
        x = lf_ref[sl, :]
        a, b, r = _split3(x)
        cc = (_dot(tri, a) + _dot(tri, b)) + _dot(tri, r) + carry
        cum_ref[sl, :] = cc
        cumt_ref[:, sl] = cc.T[:H_FOX, :]
        lft_ref[:, sl] = x.T[:H_FOX, :]
        carry = cc[CHUNK - 1:CHUNK, :]


def _cum(lfp, tri, bsz, seq):
    col = pl.BlockSpec((None, H_FOX, seq), lambda b: (b, 0, 0))
    return pl.pallas_call(
        _cum_kernel,
        grid=(bsz,),
        in_specs=[pl.BlockSpec((seq, LANES), lambda b: (b, 0)),
                  pl.BlockSpec((CHUNK, CHUNK), lambda b: (0, 0))],
        out_specs=[pl.BlockSpec((seq, LANES), lambda b: (b, 0)), col, col],
        out_shape=[jax.ShapeDtypeStruct((bsz * seq, LANES), F32),
                   jax.ShapeDtypeStruct((bsz, H_FOX, seq), F32),
                   jax.ShapeDtypeStruct((bsz, H_FOX, seq), F32)],
        compiler_params=pltpu.CompilerParams(dimension_semantics=("arbitrary",)),
        name="cum",
    )(lfp, tri)


def _fox_prompt_kernel(q_ref, kt_ref, vt_ref, cum_ref, cumt_ref, o_ref, k_sc, vt_sc,
                       *, tq, prep_chunk, pairs):
    grp = pl.program_id(1)
    i = pl.program_id(2)
    seq = kt_ref.shape[1]
    nh = 2 * pairs
    head0 = grp * nh

    @pl.when(i == 0)
    def _():
        lane = lax.broadcasted_iota(jnp.int32, (prep_chunk, LANES), 1)
        for c in range(seq // prep_chunk):
            sl = slice(c * prep_chunk, (c + 1) * prep_chunk)
            vt_sc[:, sl] = vt_ref[:, sl].astype(BF16)
            cum = cum_ref[sl, :] * LOG2E
            for pr in range(pairs):
                aux = jnp.where(lane < 3, 1.0, 0.0)
                for hh in range(2):
                    col = jnp.sum(jnp.where(lane == head0 + 2 * pr + hh, cum, 0.0),
                                  axis=-1, keepdims=True)
                    for t, piece in enumerate(_split3(col)):
                        aux = jnp.where(lane == 3 + 3 * hh + t, -piece.astype(F32), aux)
                k_rows = kt_ref[pr * LANES:(pr + 1) * LANES, sl].T
                k_sc[pr, sl, :] = jnp.concatenate([k_rows, aux], axis=1).astype(BF16)

    qt = (q_ref[...] * (ATTN_SCALE * LOG2E)).T
    q0 = pl.multiple_of(i * tq, tq)
    row = lax.broadcasted_iota(jnp.int32, (LANES, tq), 0)
    q_aug = []
    for pr in range(pairs):
        q_pair = qt[pr * LANES:(pr + 1) * LANES]
        for hh in range(2):
            cq = cumt_ref[pl.ds(head0 + 2 * pr + hh, 1), pl.ds(q0, tq)] * LOG2E
            a, b, c = (x.astype(F32) for x in _split3(cq))
            selector = jnp.where((row >= 3 + 3 * hh) & (row < 6 + 3 * hh), 1.0, 0.0)
            bottom = jnp.where(row == 0, a, jnp.where(row == 1, b, jnp.where(row == 2, c, selector)))
            top = jnp.where(row // HEAD_DIM == hh, q_pair, 0.0)
            q_aug.append(jnp.concatenate([top, bottom], axis=0).astype(BF16))

    keep = (lax.broadcasted_iota(jnp.int32, (tq, tq), 0)
            <= lax.broadcasted_iota(jnp.int32, (tq, tq), 1))

    def attend(n_keys):
        below = n_keys - tq
        strips = []
        for pr in range(pairs):
            kb = k_sc[pr, 0:n_keys, :]
            strips += [_dot(kb, q_aug[2 * pr + hh]) for hh in range(2)]
        probs = []
        for s in strips:
            s_diag = jnp.where(keep, s[below:], -jnp.inf)
            m = jnp.max(s_diag, axis=0, keepdims=True)
            if below:
                m = jnp.maximum(m, jnp.max(s[:below], axis=0, keepdims=True))
            p_diag = jnp.exp2(s_diag - m)
            l = jnp.sum(p_diag, axis=0, keepdims=True)
            p = p_diag.astype(BF16)
            if below:
                p_below = jnp.exp2(s[:below] - m)
                l = l + jnp.sum(p_below, axis=0, keepdims=True)
                p = jnp.concatenate([p_below.astype(BF16), p], axis=0)
            probs.append((p, l))
        outs = []
        for h, (p, l) in enumerate(probs):
            vt = vt_sc[h * HEAD_DIM:(h + 1) * HEAD_DIM, 0:n_keys]
            outs.append(_dot(vt, p) / l)
        o_ref[...] = jnp.concatenate(outs, axis=0).T

    for blk in range(seq // tq):
        pl.when(i == blk)(functools.partial(attend, (blk + 1) * tq))


def _fox_prompt(fq, fkt, fvt, layer, cum, cumt, bsz, seq, tq, pairs):
    nq = seq // tq
    width = pairs * LANES
    groups = FOX_W // width
    prep_chunk = _pick_tile(seq, 512)
    kv = pl.BlockSpec((None, None, width, seq), lambda b, g, i: (layer, b, g, 0))
    return pl.pallas_call(
        functools.partial(_fox_prompt_kernel, tq=tq, prep_chunk=prep_chunk, pairs=pairs),
        grid=(bsz, groups, nq),
        in_specs=[pl.BlockSpec((tq, width), lambda b, g, i: (b * nq + i, g)),
                  kv, kv,
                  pl.BlockSpec((seq, LANES), lambda b, g, i: (b, 0)),
                  pl.BlockSpec((None, H_FOX, seq), lambda b, g, i: (b, 0, 0))],
        out_specs=pl.BlockSpec((tq, width), lambda b, g, i: (b * nq + i, g)),
        out_shape=jax.ShapeDtypeStruct((bsz * seq, FOX_W), F32),
        scratch_shapes=[pltpu.VMEM((pairs, seq, 2 * LANES), BF16), pltpu.VMEM((width, seq), BF16)],
        compiler_params=pltpu.CompilerParams(
            dimension_semantics=("arbitrary", "arbitrary", "arbitrary"),
            vmem_limit_bytes=VMEM_LIMIT),
        name="fox_prompt",
    )(fq, fkt, fvt, cum, cumt)


def _fox_sample_kernel(pt_ref, q_ref, kn_ref, vn_ref, lfn_ref, w_ref, *rest, pages_per_step):
    del pt_ref
    g = pages_per_step
    k_refs = rest[:g]
    v_refs = rest[g:2 * g]
    lf_refs = rest[2 * g:3 * g]
    o_ref = rest[3 * g]
    m_sc, l_sc, acc_sc, after_sc = rest[3 * g + 1:]
    jg = pl.program_id(1)

    sub = lax.broadcasted_iota(jnp.int32, (H_FOX, FOX_W), 0)
    lane = lax.broadcasted_iota(jnp.int32, (H_FOX, FOX_W), 1)
    hmask = (lane // HEAD_DIM) == sub
    qrows = jnp.where(hmask, q_ref[...] * ATTN_SCALE, 0.0)

    @pl.when(jg == 0)
    def _():
        m_sc[...] = jnp.full_like(m_sc, NEG_BIG)
        l_sc[...] = jnp.zeros_like(l_sc)
        acc_sc[...] = jnp.zeros_like(acc_sc)
        after_sc[...] = jnp.zeros_like(after_sc)

    qb = qrows.astype(BF16)
    w = w_ref[...]
    m = m_sc[...]
    l = l_sc[...]
    after = after_sc[...]
    lf_all = jnp.concatenate([r[...] for r in lf_refs], axis=0)
    a, b, c = _split3(lf_all)
    st_all = (_dot(a, w) + _dot(b, w)) + _dot(c, w)
    scores = []
    for t in range(g):
        st = st_all[t * H_FOX:(t + 1) * H_FOX]
        kb = k_refs[t][...].astype(BF16)
        scores.append(_dot(qb, kb) + st[:, :PAGE_SIZE] + after)
        after = after + st[:, PAGE_SIZE:]
    s_max = scores[0]
    for t in range(1, g):
        s_max = jnp.maximum(s_max, scores[t])
    m_new = jnp.maximum(m, jnp.max(s_max, axis=-1, keepdims=True))
    alpha = jnp.exp(m - m_new)
    p_sum = None
    pv = None
    for t in range(g):
        p = jnp.exp(scores[t] - m_new)
        p_sum = p if p_sum is None else p_sum + p
        d = _dot_nt(p.astype(BF16), v_refs[t][...].astype(BF16))
        pv = d if pv is None else pv + d
    l = alpha * l + jnp.sum(p_sum, axis=-1, keepdims=True)
    acc = alpha * acc_sc[...] + pv
    m = m_new
    m_sc[...] = m
    l_sc[...] = l
    acc_sc[...] = acc
    after_sc[...] = after

    @pl.when(jg == pl.num_programs(1) - 1)
    def _():
        s_new = jnp.sum(qrows * kn_ref[...], axis=-1, keepdims=True) - lfn_ref[...]
        m_fin = jnp.maximum(m, s_new)
        alpha = jnp.exp(m - m_fin)
        p_new = jnp.exp(s_new - m_fin)
        l_fin = alpha * l + p_new
        acc_fin = alpha * acc + p_new * vn_ref[...]
        out = jnp.where(hmask, acc_fin / l_fin, 0.0)
        o_ref[...] = jnp.sum(out, axis=0, keepdims=True)


def _fox_sample(fq, fk, fv, lf, ck_t, cv_t, clf_t, page_table, layer, w_sfx, g):
    bsz = fq.shape[0]
    n_pages = page_table.shape[1]
    n_steps = n_pages // g
    q3 = fq.reshape(bsz, 1, FOX_W)
    k3 = fk.reshape(bsz, 1, FOX_W)
    v3 = fv.reshape(bsz, 1, FOX_W)
    lf3 = lf.reshape(bsz, H_FOX, 1)
    row = pl.BlockSpec((None, 1, FOX_W), lambda b, j, pt: (b, 0, 0))

    def page_spec(rows, t):
        def imap(b, j, pt):
            return (layer, pt[b, n_pages - 1 - (j * g + t)], 0, 0)
        return pl.BlockSpec((None, None, rows, PAGE_SIZE), imap)

    in_specs = [row, row, row,
                pl.BlockSpec((None, H_FOX, 1), lambda b, j, pt: (b, 0, 0)),
                pl.BlockSpec(w_sfx.shape, lambda b, j, pt: (0, 0))]
    in_specs += [page_spec(FOX_W, t) for t in range(g)]
    in_specs += [page_spec(FOX_W, t) for t in range(g)]
    in_specs += [page_spec(H_FOX, t) for t in range(g)]
    out = pl.pallas_call(
        functools.partial(_fox_sample_kernel, pages_per_step=g),
        grid_spec=pltpu.PrefetchScalarGridSpec(
            num_scalar_prefetch=1,
            grid=(bsz, n_steps),
            in_specs=in_specs,
            out_specs=row,
            scratch_shapes=[pltpu.VMEM((H_FOX, 1), F32), pltpu.VMEM((H_FOX, 1), F32),
                            pltpu.VMEM((H_FOX, FOX_W), F32), pltpu.VMEM((H_FOX, PAGE_SIZE), F32)]),
        out_shape=jax.ShapeDtypeStruct((bsz, 1, FOX_W), F32),
        compiler_params=pltpu.CompilerParams(
            dimension_semantics=("arbitrary", "arbitrary"), vmem_limit_bytes=VMEM_LIMIT),
        name="fox_sample",
    )(page_table, q3, k3, v3, lf3, w_sfx, *([ck_t] * g), *([cv_t] * g), *([clf_t] * g))
    return out.reshape(bsz, FOX_W)


def _merge_out(x_ref, ro_ref, rg_ref, fo_ref, y_c, seg_ref, retg_ref, wout_ref):
    y_ret = jax.nn.silu(rg_ref[...]) * _head_rms(ro_ref[...], seg_ref[...], retg_ref[...])
    y = jnp.concatenate([y_ret, fo_ref[...], y_c], axis=-1).astype(BF16)
    return x_ref[...] + _dot(y, wout_ref[...])


def _merge_prompt(x_ref, ro_ref, rg_ref, fo_ref, cu_ref, cv_ref, seg_ref, retg_ref,
                  sw_ref, sb_ref, wout_ref):
    tm = x_ref.shape[0]
    r = lax.broadcasted_iota(jnp.int32, (CHUNK, CHUNK), 0)
    c = lax.broadcasted_iota(jnp.int32, (CHUNK, CHUNK), 1)
    lane = lax.broadcasted_iota(jnp.int32, (CHUNK, CMLP_W), 1)
    ws = [jnp.where(r >= c, sw_ref[g], 0.0).astype(BF16) for g in range(C_GROUPS)]
    sb = sb_ref[...]
    chunks = []
    for n in range(tm // CHUNK):
        vb = cv_ref[n * CHUNK:(n + 1) * CHUNK, :].astype(BF16)
        mixed = sb
        for g in range(C_GROUPS):
            mixed = mixed + jnp.where((lane // HEAD_DIM) == g, _dot(ws[g], vb), 0.0)
        chunks.append(cu_ref[n * CHUNK:(n + 1) * CHUNK, :] * mixed)
    y_c = chunks[0] if len(chunks) == 1 else jnp.concatenate(chunks, axis=0)
    return _merge_out(x_ref, ro_ref, rg_ref, fo_ref, y_c, seg_ref, retg_ref, wout_ref)


def _merge_sample_kernel(x_ref, ro_ref, rg_ref, fo_ref, cu_ref, cv_ref, seg_ref, retg_ref,
                         sw0_ref, sb0_ref, wout_ref, out_ref):
    y_c = cu_ref[...] * (sw0_ref[...] * cv_ref[...] + sb0_ref[...])
    out_ref[...] = _merge_out(x_ref, ro_ref, rg_ref, fo_ref, y_c, seg_ref, retg_ref, wout_ref)


def _merge_specs(tm, d, wout, sample, row, const):
    if sample:
        sw_spec = pl.BlockSpec((1, CMLP_W), const)
        sb_spec = pl.BlockSpec((1, CMLP_W), const)
    else:
        sw_spec = pl.BlockSpec((C_GROUPS, CHUNK, CHUNK), lambda i: (0, 0, 0))
        sb_spec = pl.BlockSpec((CHUNK, CMLP_W), const)
    return [pl.BlockSpec((tm, d), row),
            pl.BlockSpec((tm, RET_W), row), pl.BlockSpec((tm, RET_W), row),
            pl.BlockSpec((tm, FOX_W), row),
            pl.BlockSpec((tm, CMLP_W), row), pl.BlockSpec((tm, CMLP_W), row),
            pl.BlockSpec((256, 256), const), pl.BlockSpec((1, RET_W), const),
            sw_spec, sb_spec,
            _layer_spec(wout, wout[0].shape[1:], const)]


def _merge_sample(x2d, ro, rg, fo, cu, cv, seg, retg, sw0, sb0, wout):
    n, d = x2d.shape
    row = lambda i: (i, 0)
    const = lambda i: (0, 0)
    return pl.pallas_call(
        _merge_sample_kernel,
        grid=(1,),
        in_specs=_merge_specs(n, d, wout, True, row, const),
        out_specs=pl.BlockSpec((n, d), row),
        out_shape=jax.ShapeDtypeStruct((n, d), F32),
        compiler_params=pltpu.CompilerParams(
            dimension_semantics=("arbitrary",), vmem_limit_bytes=VMEM_LIMIT),
        name="merge_sample",
    )(x2d, ro, rg, fo, cu, cv, seg, retg, sw0, sb0, wout[0])


def _ffn_prompt_kernel(*refs, tiles_per_seq, fb):
    merge_refs = refs[:11]
    g2_ref, wu_ref, cw_ref, cb_ref, wd_ref, prev_ref, out_ref, st_ref, act_sc, carry_sc = refs[11:]
    t = pl.program_id(0) % tiles_per_seq
    tm = out_ref.shape[0]
    d_ff = wd_ref.shape[0]

    @pl.when(t == 0)
    def _():
        carry_sc[...] = prev_ref[...]

    x = _merge_prompt(*merge_refs)
    hb = _row_rms(x, g2_ref[...]).astype(BF16)
    row = lax.broadcasted_iota(jnp.int32, (tm, fb), 0)

    def conv_cols(lo):
        u = _dot(hb, wu_ref[:, lo:lo + fb])
        prev = carry_sc[:, lo:lo + fb]
        s1 = jnp.where(row == 0, prev[1:2], pltpu.roll(u, 1, 0))
        s2 = jnp.where(row == 0, prev[0:1], jnp.where(row == 1, prev[1:2], pltpu.roll(u, 2, 0)))
        carry_sc[:, lo:lo + fb] = u[tm - (CONV_W - 1):tm, :]
        return (((cb_ref[:, lo:lo + fb] + cw_ref[0:1, lo:lo + fb] * s2)
                 + cw_ref[1:2, lo:lo + fb] * s1) + cw_ref[2:3, lo:lo + fb] * u)

    for c in range(d_ff // fb):
        gate = conv_cols(c * fb)
        val = conv_cols(d_ff + c * fb)
        act_sc[:, c * fb:(c + 1) * fb] = (jax.nn.silu(gate) * val).astype(BF16)
    st_ref[...] = carry_sc[...]
    out_ref[...] = x + _dot(act_sc[...], wd_ref[...])


def _ffn_sample_kernel(x_ref, g2_ref, wug_ref, wuv_ref, cwg_ref, cwv_ref, cbg_ref, cbv_ref, wd_ref,
                       p0g_ref, p0v_ref, p1g_ref, p1v_ref, out_ref, st_ref, h_sc, acc_sc, *, d_ff):
    j = pl.program_id(1)
    fb = wug_ref.shape[1]

    @pl.when(j == 0)
    def _():
        h_sc[...] = _row_rms(x_ref[...], g2_ref[...]).astype(BF16)
        acc_sc[...] = jnp.zeros_like(acc_sc)

    hb = h_sc[...]

    def conv(u, p0, p1, w_ref, b_ref):
        return ((b_ref[...] + w_ref[0:1] * p0) + w_ref[1:2] * p1) + w_ref[2:3] * u

    ug = _dot(hb, wug_ref[...])
    uv = _dot(hb, wuv_ref[...])
    p1g = p1g_ref[...]
    p1v = p1v_ref[...]
    conv_g = conv(ug, p0g_ref[...], p1g, cwg_ref, cbg_ref)
    conv_v = conv(uv, p0v_ref[...], p1v, cwv_ref, cbv_ref)
    cg = pl.multiple_of(j * fb, LANES)
    cv = pl.multiple_of(d_ff + j * fb, LANES)
    st_ref[:, pl.ds(cg, fb)] = p1g
    st_ref[:, pl.ds(cv, fb)] = p1v
    st_ref[:, pl.ds(pl.multiple_of(2 * d_ff + j * fb, LANES), fb)] = ug
    st_ref[:, pl.ds(pl.multiple_of(3 * d_ff + j * fb, LANES), fb)] = uv
    act = (jax.nn.silu(conv_g) * conv_v).astype(BF16)
    acc_sc[...] += _dot(act, wd_ref[...])

    @pl.when(j == pl.num_programs(1) - 1)
    def _():
        out_ref[...] = x_ref[...] + acc_sc[...]


def _ffn(x2d, g2, wup, cw, cb, wdown, prev, tm, fb, seq, merge_args=None):
    n, d = x2d.shape
    d_ff = wdown[0].shape[1]
    nff = d_ff // fb
    nt = n // tm
    xspec = pl.BlockSpec((tm, d), lambda i, j: (i, 0))
    common = [xspec,
              pl.BlockSpec((1, d), lambda i, j: (0, 0)),
              _layer_spec(wup, (d, fb), lambda i, j: (0, j)),
              _layer_spec(wup, (d, fb), lambda i, j: (0, nff + j)),
              pl.BlockSpec((CONV_W, fb), lambda i, j: (0, j)),
              pl.BlockSpec((CONV_W, fb), lambda i, j: (0, nff + j)),
              pl.BlockSpec((1, fb), lambda i, j: (0, j)),
              pl.BlockSpec((1, fb), lambda i, j: (0, nff + j)),
              _layer_spec(wdown, (fb, d), lambda i, j: (j, 0))]
    scratch = [pltpu.VMEM((tm, d), BF16), pltpu.VMEM((tm, d), F32)]
    params = pltpu.CompilerParams(
        dimension_semantics=("arbitrary", "arbitrary"), vmem_limit_bytes=VMEM_LIMIT)
    if seq == 1:
        bsz = n
        prev2 = prev.reshape(bsz, (CONV_W - 1) * 2 * d_ff)
        pspecs = [pl.BlockSpec((bsz, fb), lambda i, j, o=o: (0, o + j))
                  for o in (0, nff, 2 * nff, 3 * nff)]
        out, st = pl.pallas_call(
            functools.partial(_ffn_sample_kernel, d_ff=d_ff),
            grid=(nt, nff),
            in_specs=common + [pspecs[0], pspecs[1], pspecs[2], pspecs[3]],
            out_specs=[xspec, pl.BlockSpec(prev2.shape, lambda i, j: (0, 0))],
            out_shape=[jax.ShapeDtypeStruct((n, d), F32), jax.ShapeDtypeStruct(prev2.shape, F32)],
            scratch_shapes=scratch,
            compiler_params=params,
            name="ffn_sample",
        )(x2d, g2, wup[0], wup[0], cw, cw, cb, cb, wdown[0], prev2, prev2, prev2, prev2)
        return out, st.reshape(prev.shape)
    tiles_per_seq = seq // tm
    const = lambda i: (0, 0)
    resident = lambda w: _layer_spec(w, w[0].shape[1:], const, pipeline_mode=pl.Buffered(1))
    state = pl.BlockSpec((None, CONV_W - 1, 2 * d_ff), lambda i: (i // tiles_per_seq, 0, 0))
    row = pl.BlockSpec((tm, d), lambda i: (i, 0))
    wout = merge_args[-1]
    merge_specs = _merge_specs(tm, d, wout, False, lambda i: (i, 0), const)
    merge_specs[-1] = resident(wout)
    return pl.pallas_call(
        functools.partial(_ffn_prompt_kernel, tiles_per_seq=tiles_per_seq, fb=fb),
        grid=(nt,),
        in_specs=merge_specs + [
            pl.BlockSpec((1, d), const),
            resident(wup),
            pl.BlockSpec(cw.shape, const),
            pl.BlockSpec(cb.shape, const),
            resident(wdown),
            state],
        out_specs=[row, state],
        out_shape=[jax.ShapeDtypeStruct((n, d), F32), jax.ShapeDtypeStruct(prev.shape, F32)],
        scratch_shapes=[pltpu.VMEM((tm, d_ff), BF16), pltpu.VMEM((CONV_W - 1, 2 * d_ff), F32)],
        compiler_params=pltpu.CompilerParams(
            dimension_semantics=("arbitrary",), vmem_limit_bytes=VMEM_LIMIT),
        name="ffn_prompt",
    )(x2d, *merge_args[:-1], wout[0], g2, wup[0], cw, cb, wdown[0], prev)


def _rope_tables(pos):
    half = HEAD_DIM // 2
    inv = 1.0 / (ROPE_BASE ** (jnp.arange(half, dtype=F32) / half))
    ang = pos.astype(F32)[:, None] * inv[None, :]
    cos = jnp.cos(ang)
    sin = jnp.sin(ang)
    cos_e = jnp.tile(jnp.concatenate([cos, cos], axis=-1), (1, H_RET))
    sin_e = jnp.tile(jnp.concatenate([-sin, sin], axis=-1), (1, H_RET))
    return cos_e, sin_e


def _retention_tables():
    c = CHUNK
    lg = jnp.log1p(-jnp.exp2(-5.0 - jnp.arange(H_RET, dtype=F32)))
    idx = jnp.arange(c, dtype=F32)
    diff = idx[:, None] - idx[None, :]
    intra = jnp.where(diff[None] >= 0, jnp.exp(jnp.maximum(diff, 0.0)[None] * lg[:, None, None]), 0.0)
    q_dec = jnp.exp((idx[None, :] + 1.0) * lg[:, None])
    k_dec = jnp.exp((c - 1.0 - idx[None, :]) * lg[:, None])
    chunk_dec = jnp.exp(c * lg)
    expand = lambda a: jnp.repeat(a.T, HEAD_DIM, axis=1)
    head_of = jnp.arange(RET_W) // HEAD_DIM
    bd = (head_of[:, None] == head_of[None, :]).astype(F32)
    decmat = jnp.broadcast_to(chunk_dec[head_of][:, None], (RET_W, RET_W))
    gamma = jnp.exp(1.0 * lg)
    gcol = jnp.broadcast_to(gamma[:, None, None], (H_RET, HEAD_DIM, 1))
    grow = jnp.broadcast_to(gamma[:, None, None], (H_RET, 1, HEAD_DIM))
    return dict(intra=intra, qdec=expand(q_dec), kdec=expand(k_dec), decmat=decmat, bd=bd,
                gcol=gcol, grow=grow)


def _pick_tile(n, target):
    t = min(n, target)
    while n % t:
        t //= 2
    return t


@jax.jit
def kernel(x_prompt, x_sample, cache_k, cache_v, cache_logf, state_ret, state_ffn_conv, page_table,
           norm1_g, w_in, b_forget, ret_norm_g, q_norm_g, k_norm_g, spatial_w, spatial_b, w_out,
           norm2_g, w_up, conv_w, conv_b, w_down):
    bp, lp, d = x_prompt.shape
    bs, ls, _ = x_sample.shape
    depth = w_in.shape[0]
    d_ff = w_down.shape[1]
    n_pages = page_table.shape[1]
    past_len = n_pages * PAGE_SIZE
    assert ls == 1 and lp % CHUNK == 0

    fz0 = 4 * RET_W + 3 * FOX_W
    w_t = jnp.swapaxes(w_in, 1, 2)
    w_in_p = jnp.concatenate(
        [w_t[:, :fz0], w_t[:, fz0 + H_FOX:], w_t[:, fz0:fz0 + H_FOX],
         jnp.zeros((depth, FZ_COLS - H_FOX, d), w_in.dtype)], axis=1).astype(BF16)
    w_out_b = w_out.astype(BF16)
    w_up_b = w_up.astype(BF16)
    w_down_b = w_down.astype(BF16)
    bf_p = jnp.pad(b_forget, ((0, 0), (0, FZ_COLS - H_FOX)))
    qg_e = jnp.tile(q_norm_g, (1, H_FOX))
    kg_e = jnp.tile(k_norm_g, (1, H_FOX))
    retg_e = ret_norm_g.reshape(depth, RET_W)
    sb_e = jnp.repeat(jnp.swapaxes(spatial_b, 1, 2), HEAD_DIM, axis=2)
    sw0_e = jnp.repeat(spatial_w[:, :, 0, 0], HEAD_DIM, axis=1)
    sb0_e = jnp.repeat(spatial_b[:, :, 0], HEAD_DIM, axis=1)

    head_of = jnp.arange(256) // HEAD_DIM
    seg = (head_of[:, None] == head_of[None, :]).astype(BF16)
    idx = jnp.arange(CHUNK)
    tri_incl = (idx[:, None] >= idx[None, :]).astype(BF16)
    sfx = (idx[:, None] > idx[None, :]).astype(BF16)
    w_sfx = jnp.concatenate([sfx, jnp.ones((PAGE_SIZE, PAGE_SIZE), BF16)], axis=1)
    ck_t = jnp.transpose(cache_k, (0, 1, 3, 4, 2)).reshape(depth, -1, FOX_W, PAGE_SIZE)
    cv_t = jnp.transpose(cache_v, (0, 1, 3, 4, 2)).reshape(depth, -1, FOX_W, PAGE_SIZE)
    clf_t = jnp.transpose(cache_logf, (0, 1, 3, 2))
    cos_p, sin_p = _rope_tables(jnp.arange(lp))
    cos_s, sin_s = _rope_tables(jnp.broadcast_to(past_len + jnp.arange(ls), (bs,)))
    tabs = _retention_tables()

    tm_p = _pick_tile(lp, 512)
    tq = _pick_tile(lp, 256)
    tm_f = _pick_tile(lp, 512)
    fb = 256
    fb_s = d_ff // 2 if (d_ff // 2) % LANES == 0 else fb
    g_pages = _pick_tile(n_pages, 32)
    zeros_prev = jnp.zeros((bp, CONV_W - 1, 2 * d_ff), F32)

    xp = x_prompt.reshape(bp * lp, d)
    xs = x_sample.reshape(bs * ls, d)
    outs = [[] for _ in range(11)]
    fkt = fvt = None
    for l in range(depth):
        g1 = norm1_g[l][None]
        g2 = norm2_g[l][None]
        bfl = bf_p[l][None]
        qg = qg_e[l][None]
        kg = kg_e[l][None]
        retg = retg_e[l][None]
        cb = conv_b[l][None]

        rq, rk, rv, rg, fq, fkt, fvt, _, lfp, cu, cv = _proj(
            xp, cos_p, sin_p, g1, (w_in_p, l), bfl, qg, kg, seg, tm_p, (depth, l, fkt, fvt))
        ro, r_state = _ret_prompt(rq, rk, rv, tabs, bp, lp)
        cum, cumt, lft = _cum(lfp, tri_incl, bp, lp)
        fo = _fox_prompt(fq, fkt, fvt, l, cum, cumt, bp, lp, tq, 2)
        xp, c_state = _ffn(xp, g2, (w_up_b, l), conv_w[l], cb, (w_down_b, l), zeros_prev, tm_f, fb, lp,
                           (ro, rg, fo, cu, cv, seg, retg, spatial_w[l], sb_e[l], (w_out_b, l)))
        outs[2].append(lft)
        outs[3].append(r_state)
        outs[4].append(c_state)

        rq, rk, rv, rg, fq, fk, fv, lf, lfp, cu, cv = _proj(
            xs, cos_s, sin_s, g1, (w_in_p, l), bfl, qg, kg, seg, bs)
        ro, r_state = _ret_sample(rq, rk, rv, state_ret, l, tabs)
        fo = _fox_sample(fq, fk, fv, lf, ck_t, cv_t, clf_t, page_table, l, w_sfx, g_pages)
        xs = _merge_sample(xs, ro, rg, fo, cu, cv, seg, retg, sw0_e[l][None], sb0_e[l][None],
                           (w_out_b, l))
        xs, c_state = _ffn(xs, g2, (w_up_b, l), conv_w[l], cb, (w_down_b, l), state_ffn_conv[l], bs, fb_s, 1)
        outs[5].append(fk.reshape(bs, ls, H_FOX, HEAD_DIM))
        outs[6].append(fv.reshape(bs, ls, H_FOX, HEAD_DIM))
        outs[7].append(lf.reshape(bs, ls, H_FOX))
        outs[8].append(r_state)
        outs[9].append(cv.reshape(bs, ls, CMLP_W))
        outs[10].append(c_state)

    stacked = [jnp.stack(o) if o else None for o in outs]
    for o, buf in ((0, fkt), (1, fvt)):
        stacked[o] = jnp.transpose(buf.reshape(depth, bp, H_FOX, HEAD_DIM, lp), (0, 1, 4, 2, 3))
    stacked[2] = jnp.transpose(stacked[2], (0, 1, 3, 2))
    return (xp.reshape(bp, lp, d), xs.reshape(bs, ls, d), *stacked)
```

```python
import functools

import jax
import jax.numpy as jnp
from jax import lax
from jax.experimental import pallas as pl
from jax.experimental.pallas import tpu as pltpu

HEAD_DIM = 64
H_RET = 4
H_FOX = 8
C_GROUPS = 4
RET_W = H_RET * HEAD_DIM
FOX_W = H_FOX * HEAD_DIM
CMLP_W = C_GROUPS * HEAD_DIM
CHUNK = 128
PAGE_SIZE = 128
CONV_W = 3
RMS_EPS = 1e-6
ROPE_BASE = 10000.0
ATTN_SCALE = HEAD_DIM ** -0.5
MAIN_COLS = 4 * RET_W + 3 * FOX_W

LANES = 128
FZ_COLS = LANES
VMEM_LIMIT = 56 * 1024 * 1024
NEG_BIG = -1e30
LOG2E = 1.4426950408889634

F32 = jnp.float32
BF16 = jnp.bfloat16


def _dot(a, b):
    return jnp.dot(a, b, preferred_element_type=F32)


def _dot_nt(a, b):
    return lax.dot_general(a, b, (((1,), (1,)), ((), ())), preferred_element_type=F32)


def _split2(x):
    hi = x.astype(BF16)
    lo = (x - hi.astype(F32)).astype(BF16)
    return hi, lo


def _split3(x):
    a = x.astype(BF16)
    r = x - a.astype(F32)
    b = r.astype(BF16)
    c = (r - b.astype(F32)).astype(BF16)
    return a, b, c


def _seg_mean(sq, seg):
    hi, lo = _split2(sq)
    return (_dot(hi, seg) + _dot(lo, seg)) * (1.0 / HEAD_DIM)


def _head_rms(x, seg, g):
    parts = []
    for c in range(x.shape[-1] // 256):
        xc = x[:, c * 256:(c + 1) * 256]
        parts.append(xc * lax.rsqrt(_seg_mean(xc * xc, seg) + RMS_EPS))
    y = parts[0] if len(parts) == 1 else jnp.concatenate(parts, axis=-1)
    return y * g


def _layer_spec(stacked, block, index_map, **kwargs):
    _, layer = stacked
    return pl.BlockSpec((None,) + tuple(block), lambda *a: (layer,) + tuple(index_map(*a)), **kwargs)


def _row_rms(x, g):
    return x * lax.rsqrt(jnp.mean(x * x, axis=-1, keepdims=True) + RMS_EPS) * g


def _proj_kernel(x_ref, cos_ref, sin_ref, g1_ref, w_ref, wt_ref, bf_ref, qg_ref, kg_ref, seg_ref, *refs,
                 kv_transposed):
    (rq_ref, rk_ref, rv_ref, rg_ref, fq_ref, fk_ref, fv_ref, lf_ref, lfp_ref,
     cu_ref, cv_ref) = refs[-11:]
    hb = _row_rms(x_ref[...], g1_ref[...]).astype(BF16)
    seg = seg_ref[...]

    def proj(lo, n):
        if lo < MAIN_COLS:
            return _dot_nt(hb, w_ref[lo:lo + n, :])
        return _dot_nt(hb, wt_ref[lo - MAIN_COLS:lo - MAIN_COLS + n, :])

    cos = cos_ref[...]
    sin = sin_ref[...]
    lane = lax.broadcasted_iota(jnp.int32, cos.shape, 1)
    first_half = (lane % HEAD_DIM) < (HEAD_DIM // 2)

    def rot(z):
        swapped = jnp.where(first_half, pltpu.roll(z, RET_W - HEAD_DIM // 2, 1),
                            pltpu.roll(z, HEAD_DIM // 2, 1))
        return z * cos + swapped * sin

    rq_ref[...] = rot(proj(0, RET_W))
    rk_ref[...] = rot(proj(RET_W, RET_W)) * ATTN_SCALE
    rv_ref[...] = proj(2 * RET_W, RET_W)
    rg_ref[...] = proj(3 * RET_W, RET_W)
    off = 4 * RET_W
    fq_ref[...] = _head_rms(proj(off, FOX_W), seg, qg_ref[...])
    fk = _head_rms(proj(off + FOX_W, FOX_W), seg, kg_ref[...])
    fv = proj(off + 2 * FOX_W, FOX_W)
    fk_ref[...] = fk.T if kv_transposed else fk
    fv_ref[...] = fv.T if kv_transposed else fv
    off += 3 * FOX_W
    cu_ref[...] = jax.nn.gelu(proj(off, CMLP_W))
    cv_ref[...] = jax.nn.gelu(proj(off + CMLP_W, CMLP_W))
    off += 2 * CMLP_W
    lf = jax.nn.log_sigmoid(proj(off, FZ_COLS) + bf_ref[...])
    lane_z = lax.broadcasted_iota(jnp.int32, lf.shape, 1)
    lfp_ref[...] = jnp.where(lane_z < H_FOX, lf, 0.0)
    lf_ref[...] = lf[:, :H_FOX]


def _proj(x2d, cos, sin, g1, w, w_tail, bfp, qg, kg, seg, tm, kv_step=None):
    kv_transposed = kv_step is not None
    n, d = x2d.shape
    nt = n // tm
    tt = cos.shape[0] // tm
    row = lambda i: (i, 0)
    const = lambda i: (0, 0)
    outs = [RET_W] * 4 + [FOX_W] * 3 + [H_FOX, FZ_COLS] + [CMLP_W] * 2
    out_specs = [pl.BlockSpec((tm, c), row) for c in outs]
    out_shape = [jax.ShapeDtypeStruct((n, c), F32) for c in outs]
    in_specs = [
        pl.BlockSpec((tm, d), row),
        pl.BlockSpec((tm, RET_W), lambda i: (i % tt, 0)),
        pl.BlockSpec((tm, RET_W), lambda i: (i % tt, 0)),
        pl.BlockSpec((1, d), const),
        _layer_spec(w, (MAIN_COLS, d), const),
        _layer_spec(w_tail, w_tail[0].shape[1:], const),
        pl.BlockSpec((1, FZ_COLS), const),
        pl.BlockSpec((1, FOX_W), const),
        pl.BlockSpec((1, FOX_W), const),
        pl.BlockSpec((256, 256), const),
    ]
    args = [x2d, cos, sin, g1, w[0], w_tail[0], bfp, qg, kg, seg]
    aliases = {}
    if kv_transposed:
        depth, layer, k_buf, v_buf = kv_step
        for o in (5, 6):
            out_specs[o] = pl.BlockSpec((None, None, FOX_W, tm),
                                        lambda i: (layer, i // tt, 0, i % tt))
            out_shape[o] = jax.ShapeDtypeStruct((depth, nt // tt, FOX_W, tt * tm), F32)
        if k_buf is not None:
            aliases = {len(args): 5, len(args) + 1: 6}
            in_specs += [pl.BlockSpec(memory_space=pl.ANY)] * 2
            args += [k_buf, v_buf]
    return pl.pallas_call(
        functools.partial(_proj_kernel, kv_transposed=kv_transposed),
        grid=(nt,),
        in_specs=in_specs,
        out_specs=out_specs,
        out_shape=out_shape,
        input_output_aliases=aliases,
        compiler_params=pltpu.CompilerParams(
            dimension_semantics=("arbitrary",), vmem_limit_bytes=VMEM_LIMIT),
        name="proj",
    )(*args)


def _ret_prompt_kernel(q_ref, k_ref, v_ref, intra_ref, qdec_ref, kdec_ref, dec_ref, bd_ref,
                       o_ref, st_ref, s_sc, *, chunks):
    c = pl.program_id(1)

    @pl.when(c == 0)
    def _():
        s_sc[...] = jnp.zeros_like(s_sc)

    lane = lax.broadcasted_iota(jnp.int32, (CHUNK, RET_W), 1)
    state = s_sc[...]
    for n in range(chunks):
        sl = slice(n * CHUNK, (n + 1) * CHUNK)
        q = q_ref[sl, :]
        k = k_ref[sl, :]
        vb = v_ref[sl, :].astype(BF16)
        kb = k.astype(BF16)
        o = _dot((q * qdec_ref[...]).astype(BF16), state.astype(BF16))
        for h in range(H_RET):
            hm = (lane // HEAD_DIM) == h
            sc = _dot_nt(jnp.where(hm, q, 0.0).astype(BF16), kb) * intra_ref[h]
            o = o + jnp.where(hm, _dot(sc.astype(BF16), vb), 0.0)
        o_ref[sl, :] = o
        kd = (k * kdec_ref[...]).astype(BF16)
        ktv = lax.dot_general(kd, vb, (((0,), (0,)), ((), ())), preferred_element_type=F32)
        state = state * dec_ref[...] + bd_ref[...] * ktv
    s_sc[...] = state

    @pl.when(c == pl.num_programs(1) - 1)
    def _():
        for h in range(H_RET):
            lo = h * HEAD_DIM
            st_ref[h] = state[lo:lo + HEAD_DIM, lo:lo + HEAD_DIM]


def _ret_prompt(rq, rk, rv, tabs, bsz, seq):
    chunks = _pick_tile(seq // CHUNK, 8)
    rows = chunks * CHUNK
    ns = seq // rows
    blk = pl.BlockSpec((rows, RET_W), lambda b, c: (b * ns + c, 0))
    c2 = lambda b, c: (0, 0)
    return pl.pallas_call(
        functools.partial(_ret_prompt_kernel, chunks=chunks),
        grid=(bsz, ns),
        in_specs=[blk, blk, blk,
                  pl.BlockSpec((H_RET, CHUNK, CHUNK), lambda b, c: (0, 0, 0)),
                  pl.BlockSpec((CHUNK, RET_W), c2),
                  pl.BlockSpec((CHUNK, RET_W), c2),
                  pl.BlockSpec((RET_W, RET_W), c2),
                  pl.BlockSpec((RET_W, RET_W), c2)],
        out_specs=[blk, pl.BlockSpec((None, H_RET, HEAD_DIM, HEAD_DIM), lambda b, c: (b, 0, 0, 0))],
        out_shape=[jax.ShapeDtypeStruct((bsz * seq, RET_W), F32),
                   jax.ShapeDtypeStruct((bsz, H_RET, HEAD_DIM, HEAD_DIM), F32)],
        scratch_shapes=[pltpu.VMEM((RET_W, RET_W), F32)],
        compiler_params=pltpu.CompilerParams(
            dimension_semantics=("arbitrary", "arbitrary"), vmem_limit_bytes=VMEM_LIMIT),
        name="ret_prompt",
    )(rq, rk, rv, tabs["intra"], tabs["qdec"], tabs["kdec"], tabs["decmat"], tabs["bd"])


def _ret_sample_kernel(q_ref, k_ref, v_ref, s_ref, gcol_ref, grow_ref, o_ref, st_ref):
    for b in range(q_ref.shape[0]):
        for h in range(H_RET):
            qc = q_ref[b, h]
            kc = k_ref[b, h]
            vr = v_ref[b, h]
            s = s_ref[b, h]
            qk = jnp.sum(qc * kc, axis=0, keepdims=True)
            inter = jnp.sum((qc * gcol_ref[h]) * s, axis=0, keepdims=True)
            o_ref[b, h] = qk * vr + inter
            st_ref[b, h] = s * grow_ref[h] + kc * vr


def _ret_sample(rq, rk, rv, state, layer, tabs):
    bsz = rq.shape[0]
    nb = _pick_tile(bsz, 8)
    qc = rq.reshape(bsz, H_RET, HEAD_DIM, 1)
    kc = rk.reshape(bsz, H_RET, HEAD_DIM, 1)
    vr = rv.reshape(bsz, H_RET, 1, HEAD_DIM)
    col = pl.BlockSpec((nb, H_RET, HEAD_DIM, 1), lambda b: (b, 0, 0, 0))
    rowb = pl.BlockSpec((nb, H_RET, 1, HEAD_DIM), lambda b: (b, 0, 0, 0))
    stb = pl.BlockSpec((nb, H_RET, HEAD_DIM, HEAD_DIM), lambda b: (b, 0, 0, 0))
    o, st = pl.pallas_call(
        _ret_sample_kernel,
        grid=(bsz // nb,),
        in_specs=[col, col, rowb,
                  pl.BlockSpec((None, nb, H_RET, HEAD_DIM, HEAD_DIM), lambda b: (layer, b, 0, 0, 0)),
                  pl.BlockSpec((H_RET, HEAD_DIM, 1), lambda b: (0, 0, 0)),
                  pl.BlockSpec((H_RET, 1, HEAD_DIM), lambda b: (0, 0, 0))],
        out_specs=[rowb, stb],
        out_shape=[jax.ShapeDtypeStruct((bsz, H_RET, 1, HEAD_DIM), F32),
                   jax.ShapeDtypeStruct((bsz, H_RET, HEAD_DIM, HEAD_DIM), F32)],
        compiler_params=pltpu.CompilerParams(dimension_semantics=("arbitrary",)),
        name="ret_sample",
    )(qc, kc, vr, state, tabs["gcol"], tabs["grow"])
    return o.reshape(bsz, RET_W), st


def _cum_kernel(lf_ref, tri_ref, cum_ref, cumt_ref, lft_ref):
    tri = tri_ref[...]
    carry = jnp.zeros((1, LANES), F32)
    for c in range(lf_ref.shape[0] // CHUNK):
        sl = slice(c * CHUNK, (c + 1) * CHUNK)
        x = lf_ref[sl, :]
        a, b, r = _split3(x)
        cc = (_dot(tri, a) + _dot(tri, b)) + _dot(tri, r) + carry
        cum_ref[sl, :] = cc
        cumt_ref[:, sl] = cc.T[:H_FOX, :]
        lft_ref[:, sl] = x.T[:H_FOX, :]
        carry = cc[CHUNK - 1:CHUNK, :]


def _cum(lfp, tri, bsz, seq):
    col = pl.BlockSpec((None, H_FOX, seq), lambda b: (b, 0, 0))
    return pl.pallas_call(
        _cum_kernel,
        grid=(bsz,),
        in_specs=[pl.BlockSpec((seq, LANES), lambda b: (b, 0)),
                  pl.BlockSpec((CHUNK, CHUNK), lambda b: (0, 0))],
        out_specs=[pl.BlockSpec((seq, LANES), lambda b: (b, 0)), col, col],
        out_shape=[jax.ShapeDtypeStruct((bsz * seq, LANES), F32),
                   jax.ShapeDtypeStruct((bsz, H_FOX, seq), F32),
                   jax.ShapeDtypeStruct((bsz, H_FOX, seq), F32)],
        compiler_params=pltpu.CompilerParams(dimension_semantics=("arbitrary",)),
        name="cum",
    )(lfp, tri)


def _fox_prompt_kernel(q_ref, kt_ref, vt_ref, cum_ref, cumt_ref, o_ref, k_sc, vt_sc,
                       *, tq, prep_chunk, pairs):
    grp = pl.program_id(1)
    i = pl.program_id(2)
    seq = kt_ref.shape[1]
    nh = 2 * pairs
    head0 = grp * nh

    @pl.when(i == 0)
    def _():
        lane = lax.broadcasted_iota(jnp.int32, (prep_chunk, LANES), 1)
        for c in range(seq // prep_chunk):
            sl = slice(c * prep_chunk, (c + 1) * prep_chunk)
            vt_sc[:, sl] = vt_ref[:, sl].astype(BF16)
            cum = cum_ref[sl, :] * LOG2E
            for pr in range(pairs):
                aux = jnp.where(lane < 3, 1.0, 0.0)
                for hh in range(2):
                    col = jnp.sum(jnp.where(lane == head0 + 2 * pr + hh, cum, 0.0),
                                  axis=-1, keepdims=True)
                    for t, piece in enumerate(_split3(col)):
                        aux = jnp.where(lane == 3 + 3 * hh + t, -piece.astype(F32), aux)
                k_rows = kt_ref[pr * LANES:(pr + 1) * LANES, sl].T
                k_sc[pr, sl, :] = jnp.concatenate([k_rows, aux], axis=1).astype(BF16)

    qt = (q_ref[...] * (ATTN_SCALE * LOG2E)).T
    q0 = pl.multiple_of(i * tq, tq)
    row = lax.broadcasted_iota(jnp.int32, (LANES, tq), 0)
    q_aug = []
    for pr in range(pairs):
        q_pair = qt[pr * LANES:(pr + 1) * LANES]
        for hh in range(2):
            cq = cumt_ref[pl.ds(head0 + 2 * pr + hh, 1), pl.ds(q0, tq)] * LOG2E
            a, b, c = (x.astype(F32) for x in _split3(cq))
            selector = jnp.where((row >= 3 + 3 * hh) & (row < 6 + 3 * hh), 1.0, 0.0)
            bottom = jnp.where(row == 0, a, jnp.where(row == 1, b, jnp.where(row == 2, c, selector)))
            top = jnp.where(row // HEAD_DIM == hh, q_pair, 0.0)
            q_aug.append(jnp.concatenate([top, bottom], axis=0).astype(BF16))

    keep = (lax.broadcasted_iota(jnp.int32, (tq, tq), 0)
            <= lax.broadcasted_iota(jnp.int32, (tq, tq), 1))

    def attend(n_keys):
        below = n_keys - tq
        strips = []
        for pr in range(pairs):
            kb = k_sc[pr, 0:n_keys, :]
            strips += [_dot(kb, q_aug[2 * pr + hh]) for hh in range(2)]
        probs = []
        for s in strips:
            s_diag = jnp.where(keep, s[below:], -jnp.inf)
            m = jnp.max(s_diag, axis=0, keepdims=True)
            if below:
                m = jnp.maximum(m, jnp.max(s[:below], axis=0, keepdims=True))
            p_diag = jnp.exp2(s_diag - m)
            l = jnp.sum(p_diag, axis=0, keepdims=True)
            p = p_diag.astype(BF16)
            if below:
                p_below = jnp.exp2(s[:below] - m)
                l = l + jnp.sum(p_below, axis=0, keepdims=True)
                p = jnp.concatenate([p_below.astype(BF16), p], axis=0)
            probs.append((p, l))
        outs = []
        for h, (p, l) in enumerate(probs):
            vt = vt_sc[h * HEAD_DIM:(h + 1) * HEAD_DIM, 0:n_keys]
            outs.append(_dot(vt, p) / l)
        o_ref[...] = jnp.concatenate(outs, axis=0).T

    for blk in range(seq // tq):
        pl.when(i == blk)(functools.partial(attend, (blk + 1) * tq))


def _fox_prompt(fq, fkt, fvt, layer, cum, cumt, bsz, seq, tq, pairs):
    nq = seq // tq
    width = pairs * LANES
    groups = FOX_W // width
    prep_chunk = _pick_tile(seq, 512)
    kv = pl.BlockSpec((None, None, width, seq), lambda b, g, i: (layer, b, g, 0))
    return pl.pallas_call(
        functools.partial(_fox_prompt_kernel, tq=tq, prep_chunk=prep_chunk, pairs=pairs),
        grid=(bsz, groups, nq),
        in_specs=[pl.BlockSpec((tq, width), lambda b, g, i: (b * nq + i, g)),
                  kv, kv,
                  pl.BlockSpec((seq, LANES), lambda b, g, i: (b, 0)),
                  pl.BlockSpec((None, H_FOX, seq), lambda b, g, i: (b, 0, 0))],
        out_specs=pl.BlockSpec((tq, width), lambda b, g, i: (b * nq + i, g)),
        out_shape=jax.ShapeDtypeStruct((bsz * seq, FOX_W), F32),
        scratch_shapes=[pltpu.VMEM((pairs, seq, 2 * LANES), BF16), pltpu.VMEM((width, seq), BF16)],
        compiler_params=pltpu.CompilerParams(
            dimension_semantics=("arbitrary", "arbitrary", "arbitrary"),
            vmem_limit_bytes=VMEM_LIMIT),
        name="fox_prompt",
    )(fq, fkt, fvt, cum, cumt)


def _fox_sample_kernel(pt_ref, q_ref, kn_ref, vn_ref, lfn_ref, w_ref, *rest, pages_per_step):
    del pt_ref
    g = pages_per_step
    k_refs = rest[:g]
    v_refs = rest[g:2 * g]
    lf_refs = rest[2 * g:3 * g]
    o_ref = rest[3 * g]
    m_sc, l_sc, acc_sc, after_sc = rest[3 * g + 1:]
    jg = pl.program_id(1)

    sub = lax.broadcasted_iota(jnp.int32, (H_FOX, FOX_W), 0)
    lane = lax.broadcasted_iota(jnp.int32, (H_FOX, FOX_W), 1)
    hmask = (lane // HEAD_DIM) == sub
    qrows = jnp.where(hmask, q_ref[...] * ATTN_SCALE, 0.0)

    @pl.when(jg == 0)
    def _():
        m_sc[...] = jnp.full_like(m_sc, NEG_BIG)
        l_sc[...] = jnp.zeros_like(l_sc)
        acc_sc[...] = jnp.zeros_like(acc_sc)
        after_sc[...] = jnp.zeros_like(after_sc)

    qb = qrows.astype(BF16)
    w = w_ref[...]
    m = m_sc[...]
    l = l_sc[...]
    after = after_sc[...]
    lf_all = jnp.concatenate([r[...] for r in lf_refs], axis=0)
    a, b, c = _split3(lf_all)
    st_all = (_dot(a, w) + _dot(b, w)) + _dot(c, w)
    scores = []
    for t in range(g):
        st = st_all[t * H_FOX:(t + 1) * H_FOX]
        kb = k_refs[t][...].astype(BF16)
        scores.append(_dot(qb, kb) + st[:, :PAGE_SIZE] + after)
        after = after + st[:, PAGE_SIZE:]
    s_max = scores[0]
    for t in range(1, g):
        s_max = jnp.maximum(s_max, scores[t])
    m_new = jnp.maximum(m, jnp.max(s_max, axis=-1, keepdims=True))
    alpha = jnp.exp(m - m_new)
    p_sum = None
    pv = None
    for t in range(g):
        p = jnp.exp(scores[t] - m_new)
        p_sum = p if p_sum is None else p_sum + p
        d = _dot_nt(p.astype(BF16), v_refs[t][...].astype(BF16))
        pv = d if pv is None else pv + d
    l = alpha * l + jnp.sum(p_sum, axis=-1, keepdims=True)
    acc = alpha * acc_sc[...] + pv
    m = m_new
    m_sc[...] = m
    l_sc[...] = l
    acc_sc[...] = acc
    after_sc[...] = after

    @pl.when(jg == pl.num_programs(1) - 1)
    def _():
        s_new = jnp.sum(qrows * kn_ref[...], axis=-1, keepdims=True) - lfn_ref[...]
        m_fin = jnp.maximum(m, s_new)
        alpha = jnp.exp(m - m_fin)
        p_new = jnp.exp(s_new - m_fin)
        l_fin = alpha * l + p_new
        acc_fin = alpha * acc + p_new * vn_ref[...]
        out = jnp.where(hmask, acc_fin / l_fin, 0.0)
        o_ref[...] = jnp.sum(out, axis=0, keepdims=True)


def _fox_sample(fq, fk, fv, lf, ck_t, cv_t, clf_t, page_table, layer, w_sfx, g):
    bsz = fq.shape[0]
    n_pages = page_table.shape[1]
    n_steps = n_pages // g
    q3 = fq.reshape(bsz, 1, FOX_W)
    k3 = fk.reshape(bsz, 1, FOX_W)
    v3 = fv.reshape(bsz, 1, FOX_W)
    lf3 = lf.reshape(bsz, H_FOX, 1)
    row = pl.BlockSpec((None, 1, FOX_W), lambda b, j, pt: (b, 0, 0))

    def page_spec(rows, t):
        def imap(b, j, pt):
            return (layer, pt[b, n_pages - 1 - (j * g + t)], 0, 0)
        return pl.BlockSpec((None, None, rows, PAGE_SIZE), imap)

    in_specs = [row, row, row,
                pl.BlockSpec((None, H_FOX, 1), lambda b, j, pt: (b, 0, 0)),
                pl.BlockSpec(w_sfx.shape, lambda b, j, pt: (0, 0))]
    in_specs += [page_spec(FOX_W, t) for t in range(g)]
    in_specs += [page_spec(FOX_W, t) for t in range(g)]
    in_specs += [page_spec(H_FOX, t) for t in range(g)]
    out = pl.pallas_call(
        functools.partial(_fox_sample_kernel, pages_per_step=g),
        grid_spec=pltpu.PrefetchScalarGridSpec(
            num_scalar_prefetch=1,
            grid=(bsz, n_steps),
            in_specs=in_specs,
            out_specs=row,
            scratch_shapes=[pltpu.VMEM((H_FOX, 1), F32), pltpu.VMEM((H_FOX, 1), F32),
                            pltpu.VMEM((H_FOX, FOX_W), F32), pltpu.VMEM((H_FOX, PAGE_SIZE), F32)]),
        out_shape=jax.ShapeDtypeStruct((bsz, 1, FOX_W), F32),
        compiler_params=pltpu.CompilerParams(
            dimension_semantics=("arbitrary", "arbitrary"), vmem_limit_bytes=VMEM_LIMIT),
        name="fox_sample",
    )(page_table, q3, k3, v3, lf3, w_sfx, *([ck_t] * g), *([cv_t] * g), *([clf_t] * g))
    return out.reshape(bsz, FOX_W)


def _merge_out(x_ref, ro_ref, rg_ref, fo_ref, y_c, seg_ref, retg_ref, wout_ref):
    y_ret = jax.nn.silu(rg_ref[...]) * _head_rms(ro_ref[...], seg_ref[...], retg_ref[...])
    y = jnp.concatenate([y_ret, fo_ref[...], y_c], axis=-1).astype(BF16)
    return x_ref[...] + _dot(y, wout_ref[...])


def _merge_prompt(x_ref, ro_ref, rg_ref, fo_ref, cu_ref, cv_ref, seg_ref, retg_ref,
                  sw_ref, sb_ref, wout_ref):
    tm = x_ref.shape[0]
    r = lax.broadcasted_iota(jnp.int32, (CHUNK, CHUNK), 0)
    c = lax.broadcasted_iota(jnp.int32, (CHUNK, CHUNK), 1)
    lane = lax.broadcasted_iota(jnp.int32, (CHUNK, CMLP_W), 1)
    ws = [jnp.where(r >= c, sw_ref[g], 0.0).astype(BF16) for g in range(C_GROUPS)]
    sb = sb_ref[...]
    chunks = []
    for n in range(tm // CHUNK):
        vb = cv_ref[n * CHUNK:(n + 1) * CHUNK, :].astype(BF16)
        mixed = sb
        for g in range(C_GROUPS):
            mixed = mixed + jnp.where((lane // HEAD_DIM) == g, _dot(ws[g], vb), 0.0)
        chunks.append(cu_ref[n * CHUNK:(n + 1) * CHUNK, :] * mixed)
    y_c = chunks[0] if len(chunks) == 1 else jnp.concatenate(chunks, axis=0)
    return _merge_out(x_ref, ro_ref, rg_ref, fo_ref, y_c, seg_ref, retg_ref, wout_ref)


def _merge_sample_kernel(x_ref, ro_ref, rg_ref, fo_ref, cu_ref, cv_ref, seg_ref, retg_ref,
                         sw0_ref, sb0_ref, wout_ref, out_ref):
    y_c = cu_ref[...] * (sw0_ref[...] * cv_ref[...] + sb0_ref[...])
    out_ref[...] = _merge_out(x_ref, ro_ref, rg_ref, fo_ref, y_c, seg_ref, retg_ref, wout_ref)


def _merge_specs(tm, d, wout, sample, row, const):
    if sample:
        sw_spec = pl.BlockSpec((1, CMLP_W), const)
        sb_spec = pl.BlockSpec((1, CMLP_W), const)
    else:
        sw_spec = pl.BlockSpec((C_GROUPS, CHUNK, CHUNK), lambda i: (0, 0, 0))
        sb_spec = pl.BlockSpec((CHUNK, CMLP_W), const)
    return [pl.BlockSpec((tm, d), row),
            pl.BlockSpec((tm, RET_W), row), pl.BlockSpec((tm, RET_W), row),
            pl.BlockSpec((tm, FOX_W), row),
            pl.BlockSpec((tm, CMLP_W), row), pl.BlockSpec((tm, CMLP_W), row),
            pl.BlockSpec((256, 256), const), pl.BlockSpec((1, RET_W), const),
            sw_spec, sb_spec,
            _layer_spec(wout, wout[0].shape[1:], const)]


def _merge_sample(x2d, ro, rg, fo, cu, cv, seg, retg, sw0, sb0, wout):
    n, d = x2d.shape
    row = lambda i: (i, 0)
    const = lambda i: (0, 0)
    return pl.pallas_call(
        _merge_sample_kernel,
        grid=(1,),
        in_specs=_merge_specs(n, d, wout, True, row, const),
        out_specs=pl.BlockSpec((n, d), row),
        out_shape=jax.ShapeDtypeStruct((n, d), F32),
        compiler_params=pltpu.CompilerParams(
            dimension_semantics=("arbitrary",), vmem_limit_bytes=VMEM_LIMIT),
        name="merge_sample",
    )(x2d, ro, rg, fo, cu, cv, seg, retg, sw0, sb0, wout[0])


def _ffn_prompt_kernel(*refs, tiles_per_seq, fb):
    merge_refs = refs[:11]
    g2_ref, wu_ref, cw_ref, cb_ref, wd_ref, prev_ref, out_ref, st_ref, act_sc, carry_sc = refs[11:]
    t = pl.program_id(0) % tiles_per_seq
    tm = out_ref.shape[0]
    d_ff = wd_ref.shape[0]

    @pl.when(t == 0)
    def _():
        carry_sc[...] = prev_ref[...]

    x = _merge_prompt(*merge_refs)
    hb = _row_rms(x, g2_ref[...]).astype(BF16)
    row = lax.broadcasted_iota(jnp.int32, (tm, fb), 0)

    def conv_cols(lo):
        u = _dot(hb, wu_ref[:, lo:lo + fb])
        prev = carry_sc[:, lo:lo + fb]
        s1 = jnp.where(row == 0, prev[1:2], pltpu.roll(u, 1, 0))
        s2 = jnp.where(row == 0, prev[0:1], jnp.where(row == 1, prev[1:2], pltpu.roll(u, 2, 0)))
        carry_sc[:, lo:lo + fb] = u[tm - (CONV_W - 1):tm, :]
        return (((cb_ref[:, lo:lo + fb] + cw_ref[0:1, lo:lo + fb] * s2)
                 + cw_ref[1:2, lo:lo + fb] * s1) + cw_ref[2:3, lo:lo + fb] * u)

    for c in range(d_ff // fb):
        gate = conv_cols(c * fb)
        val = conv_cols(d_ff + c * fb)
        act_sc[:, c * fb:(c + 1) * fb] = (jax.nn.silu(gate) * val).astype(BF16)
    st_ref[...] = carry_sc[...]
    out_ref[...] = x + _dot(act_sc[...], wd_ref[...])


def _ffn_sample_kernel(x_ref, g2_ref, wug_ref, wuv_ref, cwg_ref, cwv_ref, cbg_ref, cbv_ref, wd_ref,
                       p0g_ref, p0v_ref, p1g_ref, p1v_ref, out_ref, st_ref, h_sc, acc_sc, *, d_ff):
    j = pl.program_id(1)
    fb = wug_ref.shape[1]

    @pl.when(j == 0)
    def _():
        h_sc[...] = _row_rms(x_ref[...], g2_ref[...]).astype(BF16)
        acc_sc[...] = jnp.zeros_like(acc_sc)

    hb = h_sc[...]

    def conv(u, p0, p1, w_ref, b_ref):
        return ((b_ref[...] + w_ref[0:1] * p0) + w_ref[1:2] * p1) + w_ref[2:3] * u

    ug = _dot(hb, wug_ref[...])
    uv = _dot(hb, wuv_ref[...])
    p1g = p1g_ref[...]
    p1v = p1v_ref[...]
    conv_g = conv(ug, p0g_ref[...], p1g, cwg_ref, cbg_ref)
    conv_v = conv(uv, p0v_ref[...], p1v, cwv_ref, cbv_ref)
    cg = pl.multiple_of(j * fb, LANES)
    cv = pl.multiple_of(d_ff + j * fb, LANES)
    st_ref[:, pl.ds(cg, fb)] = p1g
    st_ref[:, pl.ds(cv, fb)] = p1v
    st_ref[:, pl.ds(pl.multiple_of(2 * d_ff + j * fb, LANES), fb)] = ug
    st_ref[:, pl.ds(pl.multiple_of(3 * d_ff + j * fb, LANES), fb)] = uv
    act = (jax.nn.silu(conv_g) * conv_v).astype(BF16)
    acc_sc[...] += _dot(act, wd_ref[...])

    @pl.when(j == pl.num_programs(1) - 1)
    def _():
        out_ref[...] = x_ref[...] + acc_sc[...]


def _ffn(x2d, g2, wup, cw, cb, wdown, prev, tm, fb, seq, merge_args=None):
    n, d = x2d.shape
    d_ff = wdown[0].shape[1]
    nff = d_ff // fb
    nt = n // tm
    xspec = pl.BlockSpec((tm, d), lambda i, j: (i, 0))
    common = [xspec,
              pl.BlockSpec((1, d), lambda i, j: (0, 0)),
              _layer_spec(wup, (d, fb), lambda i, j: (0, j)),
              _layer_spec(wup, (d, fb), lambda i, j: (0, nff + j)),
              pl.BlockSpec((CONV_W, fb), lambda i, j: (0, j)),
              pl.BlockSpec((CONV_W, fb), lambda i, j: (0, nff + j)),
              pl.BlockSpec((1, fb), lambda i, j: (0, j)),
              pl.BlockSpec((1, fb), lambda i, j: (0, nff + j)),
              _layer_spec(wdown, (fb, d), lambda i, j: (j, 0))]
    scratch = [pltpu.VMEM((tm, d), BF16), pltpu.VMEM((tm, d), F32)]
    params = pltpu.CompilerParams(
        dimension_semantics=("arbitrary", "arbitrary"), vmem_limit_bytes=VMEM_LIMIT)
    if seq == 1:
        bsz = n
        prev2 = prev.reshape(bsz, (CONV_W - 1) * 2 * d_ff)
        pspecs = [pl.BlockSpec((bsz, fb), lambda i, j, o=o: (0, o + j))
                  for o in (0, nff, 2 * nff, 3 * nff)]
        out, st = pl.pallas_call(
            functools.partial(_ffn_sample_kernel, d_ff=d_ff),
            grid=(nt, nff),
            in_specs=common + [pspecs[0], pspecs[1], pspecs[2], pspecs[3]],
            out_specs=[xspec, pl.BlockSpec(prev2.shape, lambda i, j: (0, 0))],
            out_shape=[jax.ShapeDtypeStruct((n, d), F32), jax.ShapeDtypeStruct(prev2.shape, F32)],
            scratch_shapes=scratch,
            compiler_params=params,
            name="ffn_sample",
        )(x2d, g2, wup[0], wup[0], cw, cw, cb, cb, wdown[0], prev2, prev2, prev2, prev2)
        return out, st.reshape(prev.shape)
    tiles_per_seq = seq // tm
    const = lambda i: (0, 0)
    resident = lambda w: _layer_spec(w, w[0].shape[1:], const, pipeline_mode=pl.Buffered(1))
    state = pl.BlockSpec((None, CONV_W - 1, 2 * d_ff), lambda i: (i // tiles_per_seq, 0, 0))
    row = pl.BlockSpec((tm, d), lambda i: (i, 0))
    wout = merge_args[-1]
    merge_specs = _merge_specs(tm, d, wout, False, lambda i: (i, 0), const)
    merge_specs[-1] = resident(wout)
    return pl.pallas_call(
        functools.partial(_ffn_prompt_kernel, tiles_per_seq=tiles_per_seq, fb=fb),
        grid=(nt,),
        in_specs=merge_specs + [
            pl.BlockSpec((1, d), const),
            resident(wup),
            pl.BlockSpec(cw.shape, const),
            pl.BlockSpec(cb.shape, const),
            resident(wdown),
            state],
        out_specs=[row, state],
        out_shape=[jax.ShapeDtypeStruct((n, d), F32), jax.ShapeDtypeStruct(prev.shape, F32)],
        scratch_shapes=[pltpu.VMEM((tm, d_ff), BF16), pltpu.VMEM((CONV_W - 1, 2 * d_ff), F32)],
        compiler_params=pltpu.CompilerParams(
            dimension_semantics=("arbitrary",), vmem_limit_bytes=VMEM_LIMIT),
        name="ffn_prompt",
    )(x2d, *merge_args[:-1], wout[0], g2, wup[0], cw, cb, wdown[0], prev)


def _rope_tables(pos):
    half = HEAD_DIM // 2
    inv = 1.0 / (ROPE_BASE ** (jnp.arange(half, dtype=F32) / half))
    ang = pos.astype(F32)[:, None] * inv[None, :]
    cos = jnp.cos(ang)
    sin = jnp.sin(ang)
    cos_e = jnp.tile(jnp.concatenate([cos, cos], axis=-1), (1, H_RET))
    sin_e = jnp.tile(jnp.concatenate([-sin, sin], axis=-1), (1, H_RET))
    return cos_e, sin_e


def _retention_tables():
    c = CHUNK
    lg = jnp.log1p(-jnp.exp2(-5.0 - jnp.arange(H_RET, dtype=F32)))
    idx = jnp.arange(c, dtype=F32)
    diff = idx[:, None] - idx[None, :]
    intra = jnp.where(diff[None] >= 0, jnp.exp(jnp.maximum(diff, 0.0)[None] * lg[:, None, None]), 0.0)
    q_dec = jnp.exp((idx[None, :] + 1.0) * lg[:, None])
    k_dec = jnp.exp((c - 1.0 - idx[None, :]) * lg[:, None])
    chunk_dec = jnp.exp(c * lg)
    expand = lambda a: jnp.repeat(a.T, HEAD_DIM, axis=1)
    head_of = jnp.arange(RET_W) // HEAD_DIM
    bd = (head_of[:, None] == head_of[None, :]).astype(F32)
    decmat = jnp.broadcast_to(chunk_dec[head_of][:, None], (RET_W, RET_W))
    gamma = jnp.exp(1.0 * lg)
    gcol = jnp.broadcast_to(gamma[:, None, None], (H_RET, HEAD_DIM, 1))
    grow = jnp.broadcast_to(gamma[:, None, None], (H_RET, 1, HEAD_DIM))
    return dict(intra=intra, qdec=expand(q_dec), kdec=expand(k_dec), decmat=decmat, bd=bd,
                gcol=gcol, grow=grow)


def _pick_tile(n, target):
    t = min(n, target)
    while n % t:
        t //= 2
    return t


@jax.jit
def kernel(x_prompt, x_sample, cache_k, cache_v, cache_logf, state_ret, state_ffn_conv, page_table,
           norm1_g, w_in, b_forget, ret_norm_g, q_norm_g, k_norm_g, spatial_w, spatial_b, w_out,
           norm2_g, w_up, conv_w, conv_b, w_down):
    bp, lp, d = x_prompt.shape
    bs, ls, _ = x_sample.shape
    depth = w_in.shape[0]
    d_ff = w_down.shape[1]
    n_pages = page_table.shape[1]
    past_len = n_pages * PAGE_SIZE
    assert ls == 1 and lp % CHUNK == 0

    fz0 = MAIN_COLS
    w_t = jnp.swapaxes(w_in, 1, 2)
    w_in_b = w_t.astype(BF16)
    w_tail = jnp.concatenate(
        [w_t[:, fz0 + H_FOX:], w_t[:, fz0:fz0 + H_FOX],
         jnp.zeros((depth, FZ_COLS - H_FOX, d), w_in.dtype)], axis=1).astype(BF16)
    w_out_b = w_out.astype(BF16)
    w_up_b = w_up.astype(BF16)
    w_down_b = w_down.astype(BF16)
    bf_p = jnp.pad(b_forget, ((0, 0), (0, FZ_COLS - H_FOX)))
    qg_e = jnp.tile(q_norm_g, (1, H_FOX))
    kg_e = jnp.tile(k_norm_g, (1, H_FOX))
    retg_e = ret_norm_g.reshape(depth, RET_W)
    sb_e = jnp.repeat(jnp.swapaxes(spatial_b, 1, 2), HEAD_DIM, axis=2)
    sw0_e = jnp.repeat(spatial_w[:, :, 0, 0], HEAD_DIM, axis=1)
    sb0_e = jnp.repeat(spatial_b[:, :, 0], HEAD_DIM, axis=1)

    head_of = jnp.arange(256) // HEAD_DIM
    seg = (head_of[:, None] == head_of[None, :]).astype(BF16)
    idx = jnp.arange(CHUNK)
    tri_incl = (idx[:, None] >= idx[None, :]).astype(BF16)
    sfx = (idx[:, None] > idx[None, :]).astype(BF16)
    w_sfx = jnp.concatenate([sfx, jnp.ones((PAGE_SIZE, PAGE_SIZE), BF16)], axis=1)
    ck_t = jnp.transpose(cache_k, (0, 1, 3, 4, 2)).reshape(depth, -1, FOX_W, PAGE_SIZE)
    cv_t = jnp.transpose(cache_v, (0, 1, 3, 4, 2)).reshape(depth, -1, FOX_W, PAGE_SIZE)
    clf_t = jnp.transpose(cache_logf, (0, 1, 3, 2))
    cos_p, sin_p = _rope_tables(jnp.arange(lp))
    cos_s, sin_s = _rope_tables(jnp.broadcast_to(past_len + jnp.arange(ls), (bs,)))
    tabs = _retention_tables()

    tm_p = _pick_tile(lp, 512)
    tq = _pick_tile(lp, 256)
    tm_f = _pick_tile(lp, 512)
    fb = 256
    fb_s = d_ff // 2 if (d_ff // 2) % LANES == 0 else fb
    g_pages = _pick_tile(n_pages, 32)
    zeros_prev = jnp.zeros((bp, CONV_W - 1, 2 * d_ff), F32)

    xp = x_prompt.reshape(bp * lp, d)
    xs = x_sample.reshape(bs * ls, d)
    outs = [[] for _ in range(11)]
    fkt = fvt = None
    for l in range(depth):
        g1 = norm1_g[l][None]
        g2 = norm2_g[l][None]
        bfl = bf_p[l][None]
        qg = qg_e[l][None]
        kg = kg_e[l][None]
        retg = retg_e[l][None]
        cb = conv_b[l][None]

        rq, rk, rv, rg, fq, fkt, fvt, _, lfp, cu, cv = _proj(
            xp, cos_p, sin_p, g1, (w_in_b, l), (w_tail, l), bfl, qg, kg, seg, tm_p, (depth, l, fkt, fvt))
        ro, r_state = _ret_prompt(rq, rk, rv, tabs, bp, lp)
        cum, cumt, lft = _cum(lfp, tri_incl, bp, lp)
        fo = _fox_prompt(fq, fkt, fvt, l, cum, cumt, bp, lp, tq, 2)
        xp, c_state = _ffn(xp, g2, (w_up_b, l), conv_w[l], cb, (w_down_b, l), zeros_prev, tm_f, fb, lp,
                           (ro, rg, fo, cu, cv, seg, retg, spatial_w[l], sb_e[l], (w_out_b, l)))
        outs[2].append(lft)
        outs[3].append(r_state)
        outs[4].append(c_state)

        rq, rk, rv, rg, fq, fk, fv, lf, lfp, cu, cv = _proj(
            xs, cos_s, sin_s, g1, (w_in_b, l), (w_tail, l), bfl, qg, kg, seg, bs)
        ro, r_state = _ret_sample(rq, rk, rv, state_ret, l, tabs)
        fo = _fox_sample(fq, fk, fv, lf, ck_t, cv_t, clf_t, page_table, l, w_sfx, g_pages)
        xs = _merge_sample(xs, ro, rg, fo, cu, cv, seg, retg, sw0_e[l][None], sb0_e[l][None],
                           (w_out_b, l))
        xs, c_state = _ffn(xs, g2, (w_up_b, l), conv_w[l], cb, (w_down_b, l), state_ffn_conv[l], bs, fb_s, 1)
        outs[5].append(fk.reshape(bs, ls, H_FOX, HEAD_DIM))
        outs[6].append(fv.reshape(bs, ls, H_FOX, HEAD_DIM))
        outs[7].append(lf.reshape(bs, ls, H_FOX))
        outs[8].append(r_state)
        outs[9].append(cv.reshape(bs, ls, CMLP_W))
        outs[10].append(c_state)

    stacked = [jnp.stack(o) if o else None for o in outs]
    for o, buf in ((0, fkt), (1, fvt)):
        stacked[o] = jnp.transpose(buf.reshape(depth, bp, H_FOX, HEAD_DIM, lp), (0, 1, 4, 2, 3))
    stacked[2] = jnp.transpose(stacked[2], (0, 1, 3, 2))
    return (xp.reshape(bp, lp, d), xs.reshape(bs, ls, d), *stacked)
```

```python
import functools

import jax
import jax.numpy as jnp
from jax import lax
from jax.experimental import pallas as pl
from jax.experimental.pallas import tpu as pltpu

HEAD_DIM = 64
H_RET = 4
H_FOX = 8
C_GROUPS = 4
RET_W = H_RET * HEAD_DIM
FOX_W = H_FOX * HEAD_DIM
CMLP_W = C_GROUPS * HEAD_DIM
CHUNK = 128
PAGE_SIZE = 128
CONV_W = 3
RMS_EPS = 1e-6
ROPE_BASE = 10000.0
ATTN_SCALE = HEAD_DIM ** -0.5
MAIN_COLS = 4 * RET_W + 3 * FOX_W

LANES = 128
FZ_COLS = LANES
VMEM_LIMIT = 56 * 1024 * 1024
NEG_BIG = -1e30
LOG2E = 1.4426950408889634

F32 = jnp.float32
BF16 = jnp.bfloat16


def _dot(a, b):
    return jnp.dot(a, b, preferred_element_type=F32)


def _dot_nt(a, b):
    return lax.dot_general(a, b, (((1,), (1,)), ((), ())), preferred_element_type=F32)


def _split2(x):
    hi = x.astype(BF16)
    lo = (x - hi.astype(F32)).astype(BF16)
    return hi, lo


def _split3(x):
    a = x.astype(BF16)
    r = x - a.astype(F32)
    b = r.astype(BF16)
    c = (r - b.astype(F32)).astype(BF16)
    return a, b, c


def _seg_mean(sq, seg):
    hi, lo = _split2(sq)
    return (_dot(hi, seg) + _dot(lo, seg)) * (1.0 / HEAD_DIM)


def _head_rms(x, seg, g):
    parts = []
    for c in range(x.shape[-1] // 256):
        xc = x[:, c * 256:(c + 1) * 256]
        parts.append(xc * lax.rsqrt(_seg_mean(xc * xc, seg) + RMS_EPS))
    y = parts[0] if len(parts) == 1 else jnp.concatenate(parts, axis=-1)
    return y * g


def _layer_spec(stacked, block, index_map, **kwargs):
    _, layer = stacked
    return pl.BlockSpec((None,) + tuple(block), lambda *a: (layer,) + tuple(index_map(*a)), **kwargs)


def _row_rms(x, g):
    return x * lax.rsqrt(jnp.mean(x * x, axis=-1, keepdims=True) + RMS_EPS) * g


def _proj_kernel(x_ref, cos_ref, sin_ref, g1_ref, w_ref, wt_ref, bf_ref, qg_ref, kg_ref, seg_ref, *refs,
                 kv_transposed):
    (rq_ref, rk_ref, rv_ref, rg_ref, fq_ref, fk_ref, fv_ref, lf_ref, lfp_ref,
     cu_ref, cv_ref) = refs[-11:]
    hb = _row_rms(x_ref[...], g1_ref[...]).astype(BF16)
    seg = seg_ref[...]

    def proj(lo, n):
        if lo < MAIN_COLS:
            return _dot_nt(hb, w_ref[lo:lo + n, :])
        return _dot_nt(hb, wt_ref[lo - MAIN_COLS:lo - MAIN_COLS + n, :])

    cos = cos_ref[...]
    sin = sin_ref[...]
    lane = lax.broadcasted_iota(jnp.int32, cos.shape, 1)
    first_half = (lane % HEAD_DIM) < (HEAD_DIM // 2)

    def rot(z):
        swapped = jnp.where(first_half, pltpu.roll(z, RET_W - HEAD_DIM // 2, 1),
                            pltpu.roll(z, HEAD_DIM // 2, 1))
        return z * cos + swapped * sin

    rq_ref[...] = rot(proj(0, RET_W))
    rk_ref[...] = rot(proj(RET_W, RET_W)) * ATTN_SCALE
    rv_ref[...] = proj(2 * RET_W, RET_W)
    rg_ref[...] = proj(3 * RET_W, RET_W)
    off = 4 * RET_W
    fq_ref[...] = _head_rms(proj(off, FOX_W), seg, qg_ref[...])
    fk = _head_rms(proj(off + FOX_W, FOX_W), seg, kg_ref[...])
    fv = proj(off + 2 * FOX_W, FOX_W)
    fk_ref[...] = fk.T if kv_transposed else fk
    fv_ref[...] = fv.T if kv_transposed else fv
    off += 3 * FOX_W
    cu_ref[...] = jax.nn.gelu(proj(off, CMLP_W))
    cv_ref[...] = jax.nn.gelu(proj(off + CMLP_W, CMLP_W))
    off += 2 * CMLP_W
    lf = jax.nn.log_sigmoid(proj(off, FZ_COLS) + bf_ref[...])
    lane_z = lax.broadcasted_iota(jnp.int32, lf.shape, 1)
    lfp_ref[...] = jnp.where(lane_z < H_FOX, lf, 0.0)
    lf_ref[...] = lf[:, :H_FOX]


def _proj(x2d, cos, sin, g1, w, w_tail, bfp, qg, kg, seg, tm, kv_step=None):
    kv_transposed = kv_step is not None
    n, d = x2d.shape
    nt = n // tm
    tt = cos.shape[0] // tm
    row = lambda i: (i, 0)
    const = lambda i: (0, 0)
    outs = [RET_W] * 4 + [FOX_W] * 3 + [H_FOX, FZ_COLS] + [CMLP_W] * 2
    out_specs = [pl.BlockSpec((tm, c), row) for c in outs]
    out_shape = [jax.ShapeDtypeStruct((n, c), F32) for c in outs]
    in_specs = [
        pl.BlockSpec((tm, d), row),
        pl.BlockSpec((tm, RET_W), lambda i: (i % tt, 0)),
        pl.BlockSpec((tm, RET_W), lambda i: (i % tt, 0)),
        pl.BlockSpec((1, d), const),
        _layer_spec(w, (MAIN_COLS, d), const),
        _layer_spec(w_tail, w_tail[0].shape[1:], const),
        pl.BlockSpec((1, FZ_COLS), const),
        pl.BlockSpec((1, FOX_W), const),
        pl.BlockSpec((1, FOX_W), const),
        pl.BlockSpec((256, 256), const),
    ]
    args = [x2d, cos, sin, g1, w[0], w_tail[0], bfp, qg, kg, seg]
    aliases = {}
    if kv_transposed:
        depth, layer, k_buf, v_buf = kv_step
        for o in (5, 6):
            out_specs[o] = pl.BlockSpec((None, None, FOX_W, tm),
                                        lambda i: (layer, i // tt, 0, i % tt))
            out_shape[o] = jax.ShapeDtypeStruct((depth, nt // tt, FOX_W, tt * tm), F32)
        if k_buf is not None:
            aliases = {len(args): 5, len(args) + 1: 6}
            in_specs += [pl.BlockSpec(memory_space=pl.ANY)] * 2
            args += [k_buf, v_buf]
    return pl.pallas_call(
        functools.partial(_proj_kernel, kv_transposed=kv_transposed),
        grid=(nt,),
        in_specs=in_specs,
        out_specs=out_specs,
        out_shape=out_shape,
        input_output_aliases=aliases,
        compiler_params=pltpu.CompilerParams(
            dimension_semantics=("arbitrary",), vmem_limit_bytes=VMEM_LIMIT),
        name="proj",
    )(*args)


def _ret_prompt_kernel(q_ref, k_ref, v_ref, intra_ref, qdec_ref, kdec_ref, dec_ref, bd_ref,
                       o_ref, st_ref, s_sc, *, chunks):
    c = pl.program_id(1)

    @pl.when(c == 0)
    def _():
        s_sc[...] = jnp.zeros_like(s_sc)

    lane = lax.broadcasted_iota(jnp.int32, (CHUNK, RET_W), 1)
    state = s_sc[...]
    for n in range(chunks):
        sl = slice(n * CHUNK, (n + 1) * CHUNK)
        q = q_ref[sl, :]
        k = k_ref[sl, :]
        vb = v_ref[sl, :].astype(BF16)
        kb = k.astype(BF16)
        o = _dot((q * qdec_ref[...]).astype(BF16), state.astype(BF16))
        for h in range(H_RET):
            hm = (lane // HEAD_DIM) == h
            sc = _dot_nt(jnp.where(hm, q, 0.0).astype(BF16), kb) * intra_ref[h]
            o = o + jnp.where(hm, _dot(sc.astype(BF16), vb), 0.0)
        o_ref[sl, :] = o
        kd = (k * kdec_ref[...]).astype(BF16)
        ktv = lax.dot_general(kd, vb, (((0,), (0,)), ((), ())), preferred_element_type=F32)
        state = state * dec_ref[...] + bd_ref[...] * ktv
    s_sc[...] = state

    @pl.when(c == pl.num_programs(1) - 1)
    def _():
        for h in range(H_RET):
            lo = h * HEAD_DIM
            st_ref[h] = state[lo:lo + HEAD_DIM, lo:lo + HEAD_DIM]


def _ret_prompt(rq, rk, rv, tabs, bsz, seq):
    chunks = _pick_tile(seq // CHUNK, 8)
    rows = chunks * CHUNK
    ns = seq // rows
    blk = pl.BlockSpec((rows, RET_W), lambda b, c: (b * ns + c, 0))
    c2 = lambda b, c: (0, 0)
    return pl.pallas_call(
        functools.partial(_ret_prompt_kernel, chunks=chunks),
        grid=(bsz, ns),
        in_specs=[blk, blk, blk,
                  pl.BlockSpec((H_RET, CHUNK, CHUNK), lambda b, c: (0, 0, 0)),
                  pl.BlockSpec((CHUNK, RET_W), c2),
                  pl.BlockSpec((CHUNK, RET_W), c2),
                  pl.BlockSpec((RET_W, RET_W), c2),
                  pl.BlockSpec((RET_W, RET_W), c2)],
        out_specs=[blk, pl.BlockSpec((None, H_RET, HEAD_DIM, HEAD_DIM), lambda b, c: (b, 0, 0, 0))],
        out_shape=[jax.ShapeDtypeStruct((bsz * seq, RET_W), F32),
                   jax.ShapeDtypeStruct((bsz, H_RET, HEAD_DIM, HEAD_DIM), F32)],
        scratch_shapes=[pltpu.VMEM((RET_W, RET_W), F32)],
        compiler_params=pltpu.CompilerParams(
            dimension_semantics=("arbitrary", "arbitrary"), vmem_limit_bytes=VMEM_LIMIT),
        name="ret_prompt",
    )(rq, rk, rv, tabs["intra"], tabs["qdec"], tabs["kdec"], tabs["decmat"], tabs["bd"])


def _ret_sample_kernel(q_ref, k_ref, v_ref, s_ref, gcol_ref, grow_ref, o_ref, st_ref):
    for b in range(q_ref.shape[0]):
        for h in range(H_RET):
            qc = q_ref[b, h]
            kc = k_ref[b, h]
            vr = v_ref[b, h]
            s = s_ref[b, h]
            qk = jnp.sum(qc * kc, axis=0, keepdims=True)
            inter = jnp.sum((qc * gcol_ref[h]) * s, axis=0, keepdims=True)
            o_ref[b, h] = qk * vr + inter
            st_ref[b, h] = s * grow_ref[h] + kc * vr


def _ret_sample(rq, rk, rv, state, layer, tabs):
    bsz = rq.shape[0]
    nb = _pick_tile(bsz, 8)
    qc = rq.reshape(bsz, H_RET, HEAD_DIM, 1)
    kc = rk.reshape(bsz, H_RET, HEAD_DIM, 1)
    vr = rv.reshape(bsz, H_RET, 1, HEAD_DIM)
    col = pl.BlockSpec((nb, H_RET, HEAD_DIM, 1), lambda b: (b, 0, 0, 0))
    rowb = pl.BlockSpec((nb, H_RET, 1, HEAD_DIM), lambda b: (b, 0, 0, 0))
    stb = pl.BlockSpec((nb, H_RET, HEAD_DIM, HEAD_DIM), lambda b: (b, 0, 0, 0))
    o, st = pl.pallas_call(
        _ret_sample_kernel,
        grid=(bsz // nb,),
        in_specs=[col, col, rowb,
                  pl.BlockSpec((None, nb, H_RET, HEAD_DIM, HEAD_DIM), lambda b: (layer, b, 0, 0, 0)),
                  pl.BlockSpec((H_RET, HEAD_DIM, 1), lambda b: (0, 0, 0)),
                  pl.BlockSpec((H_RET, 1, HEAD_DIM), lambda b: (0, 0, 0))],
        out_specs=[rowb, stb],
        out_shape=[jax.ShapeDtypeStruct((bsz, H_RET, 1, HEAD_DIM), F32),
                   jax.ShapeDtypeStruct((bsz, H_RET, HEAD_DIM, HEAD_DIM), F32)],
        compiler_params=pltpu.CompilerParams(dimension_semantics=("arbitrary",)),
        name="ret_sample",
    )(qc, kc, vr, state, tabs["gcol"], tabs["grow"])
    return o.reshape(bsz, RET_W), st


def _cum_kernel(lf_ref, tri_ref, cum_ref, cumt_ref, lft_ref):
    tri = tri_ref[...]
    carry = jnp.zeros((1, LANES), F32)
    for c in range(lf_ref.shape[0] // CHUNK):
        sl = slice(c * CHUNK, (c + 1) * CHUNK)
        x = lf_ref[sl, :]
        a, b, r = _split3(x)
        cc = (_dot(tri, a) + _dot(tri, b)) + _dot(tri, r) + carry
        cum_ref[sl, :] = cc
        cumt_ref[:, sl] = cc.T[:H_FOX, :]
        lft_ref[:, sl] = x.T[:H_FOX, :]
        carry = cc[CHUNK - 1:CHUNK, :]


def _cum(lfp, tri, bsz, seq):
    col = pl.BlockSpec((None, H_FOX, seq), lambda b: (b, 0, 0))
    return pl.pallas_call(
        _cum_kernel,
        grid=(bsz,),
        in_specs=[pl.BlockSpec((seq, LANES), lambda b: (b, 0)),
                  pl.BlockSpec((CHUNK, CHUNK), lambda b: (0, 0))],
        out_specs=[pl.BlockSpec((seq, LANES), lambda b: (b, 0)), col, col],
        out_shape=[jax.ShapeDtypeStruct((bsz * seq, LANES), F32),
                   jax.ShapeDtypeStruct((bsz, H_FOX, seq), F32),
                   jax.ShapeDtypeStruct((bsz, H_FOX, seq), F32)],
        compiler_params=pltpu.CompilerParams(dimension_semantics=("arbitrary",)),
        name="cum",
    )(lfp, tri)


def _fox_prompt_kernel(q_ref, kt_ref, vt_ref, cum_ref, cumt_ref, o_ref, k_sc, vt_sc,
                       *, tq, prep_chunk, pairs):
    grp = pl.program_id(1)
    i = pl.program_id(2)
    seq = kt_ref.shape[1]
    nh = 2 * pairs
    head0 = grp * nh

    @pl.when(i == 0)
    def _():
        row = lax.broadcasted_iota(jnp.int32, (LANES, prep_chunk), 0)
        for c in range(seq // prep_chunk):
            sl = slice(c * prep_chunk, (c + 1) * prep_chunk)
            vt_sc[:, sl] = vt_ref[:, sl].astype(BF16)
            for pr in range(pairs):
                aux_t = jnp.where(row < 3, 1.0, 0.0)
                for hh in range(2):
                    ck = cumt_ref[pl.ds(head0 + 2 * pr + hh, 1), sl] * LOG2E
                    for t, piece in enumerate(_split3(ck)):
                        aux_t = jnp.where(row == 3 + 3 * hh + t, -piece.astype(F32), aux_t)
                k_t = kt_ref[pr * LANES:(pr + 1) * LANES, sl]
                k_sc[pr, sl, :] = jnp.concatenate([k_t, aux_t], axis=0).T.astype(BF16)

    qt = (q_ref[...] * (ATTN_SCALE * LOG2E)).T
    q0 = pl.multiple_of(i * tq, tq)
    row = lax.broadcasted_iota(jnp.int32, (LANES, tq), 0)
    q_aug = []
    for pr in range(pairs):
        q_pair = qt[pr * LANES:(pr + 1) * LANES]
        for hh in range(2):
            cq = cumt_ref[pl.ds(head0 + 2 * pr + hh, 1), pl.ds(q0, tq)] * LOG2E
            a, b, c = (x.astype(F32) for x in _split3(cq))
            selector = jnp.where((row >= 3 + 3 * hh) & (row < 6 + 3 * hh), 1.0, 0.0)
            bottom = jnp.where(row == 0, a, jnp.where(row == 1, b, jnp.where(row == 2, c, selector)))
            top = jnp.where(row // HEAD_DIM == hh, q_pair, 0.0)
            q_aug.append(jnp.concatenate([top, bottom], axis=0).astype(BF16))

    keep = (lax.broadcasted_iota(jnp.int32, (tq, tq), 0)
            <= lax.broadcasted_iota(jnp.int32, (tq, tq), 1))

    def attend(n_keys):
        below = n_keys - tq
        strips = []
        for pr in range(pairs):
            kb = k_sc[pr, 0:n_keys, :]
            strips += [_dot(kb, q_aug[2 * pr + hh]) for hh in range(2)]
        probs = []
        for s in strips:
            s_diag = jnp.where(keep, s[below:], -jnp.inf)
            m = jnp.max(s_diag, axis=0, keepdims=True)
            if below:
                m = jnp.maximum(m, jnp.max(s[:below], axis=0, keepdims=True))
            p_diag = jnp.exp2(s_diag - m)
            l = jnp.sum(p_diag, axis=0, keepdims=True)
            p = p_diag.astype(BF16)
            if below:
                p_below = jnp.exp2(s[:below] - m)
                l = l + jnp.sum(p_below, axis=0, keepdims=True)
                p = jnp.concatenate([p_below.astype(BF16), p], axis=0)
            probs.append((p, l))
        outs = []
        for h, (p, l) in enumerate(probs):
            vt = vt_sc[h * HEAD_DIM:(h + 1) * HEAD_DIM, 0:n_keys]
            outs.append(_dot(vt, p) / l)
        o_ref[...] = jnp.concatenate(outs, axis=0).T

    for blk in range(seq // tq):
        pl.when(i == blk)(functools.partial(attend, (blk + 1) * tq))


def _fox_prompt(fq, fkt, fvt, layer, cum, cumt, bsz, seq, tq, pairs):
    nq = seq // tq
    width = pairs * LANES
    groups = FOX_W // width
    prep_chunk = _pick_tile(seq, 512)
    kv = pl.BlockSpec((None, None, width, seq), lambda b, g, i: (layer, b, g, 0))
    return pl.pallas_call(
        functools.partial(_fox_prompt_kernel, tq=tq, prep_chunk=prep_chunk, pairs=pairs),
        grid=(bsz, groups, nq),
        in_specs=[pl.BlockSpec((tq, width), lambda b, g, i: (b * nq + i, g)),
                  kv, kv,
                  pl.BlockSpec((seq, LANES), lambda b, g, i: (b, 0)),
                  pl.BlockSpec((None, H_FOX, seq), lambda b, g, i: (b, 0, 0))],
        out_specs=pl.BlockSpec((tq, width), lambda b, g, i: (b * nq + i, g)),
        out_shape=jax.ShapeDtypeStruct((bsz * seq, FOX_W), F32),
        scratch_shapes=[pltpu.VMEM((pairs, seq, 2 * LANES), BF16), pltpu.VMEM((width, seq), BF16)],
        compiler_params=pltpu.CompilerParams(
            dimension_semantics=("arbitrary", "arbitrary", "arbitrary"),
            vmem_limit_bytes=VMEM_LIMIT),
        name="fox_prompt",
    )(fq, fkt, fvt, cum, cumt)


def _fox_sample_kernel(pt_ref, q_ref, kn_ref, vn_ref, lfn_ref, w_ref, *rest, pages_per_step):
    del pt_ref
    g = pages_per_step
    k_refs = rest[:g]
    v_refs = rest[g:2 * g]
    lf_refs = rest[2 * g:3 * g]
    o_ref = rest[3 * g]
    m_sc, l_sc, acc_sc, after_sc = rest[3 * g + 1:]
    jg = pl.program_id(1)

    sub = lax.broadcasted_iota(jnp.int32, (H_FOX, FOX_W), 0)
    lane = lax.broadcasted_iota(jnp.int32, (H_FOX, FOX_W), 1)
    hmask = (lane // HEAD_DIM) == sub
    qrows = jnp.where(hmask, q_ref[...] * ATTN_SCALE, 0.0)

    @pl.when(jg == 0)
    def _():
        m_sc[...] = jnp.full_like(m_sc, NEG_BIG)
        l_sc[...] = jnp.zeros_like(l_sc)
        acc_sc[...] = jnp.zeros_like(acc_sc)
        after_sc[...] = jnp.zeros_like(after_sc)

    qb = qrows.astype(BF16)
    w = w_ref[...]
    m = m_sc[...]
    l = l_sc[...]
    after = after_sc[...]
    lf_all = jnp.concatenate([r[...] for r in lf_refs], axis=0)
    a, b, c = _split3(lf_all)
    st_all = (_dot(a, w) + _dot(b, w)) + _dot(c, w)
    scores = []
    for t in range(g):
        st = st_all[t * H_FOX:(t + 1) * H_FOX]
        kb = k_refs[t][...].astype(BF16)
        scores.append(_dot(qb, kb) + st[:, :PAGE_SIZE] + after)
        after = after + st[:, PAGE_SIZE:]
    s_max = scores[0]
    for t in range(1, g):
        s_max = jnp.maximum(s_max, scores[t])
    m_new = jnp.maximum(m, jnp.max(s_max, axis=-1, keepdims=True))
    alpha = jnp.exp(m - m_new)
    p_sum = None
    pv = None
    for t in range(g):
        p = jnp.exp(scores[t] - m_new)
        p_sum = p if p_sum is None else p_sum + p
        d = _dot_nt(p.astype(BF16), v_refs[t][...].astype(BF16))
        pv = d if pv is None else pv + d
    l = alpha * l + jnp.sum(p_sum, axis=-1, keepdims=True)
    acc = alpha * acc_sc[...] + pv
    m = m_new
    m_sc[...] = m
    l_sc[...] = l
    acc_sc[...] = acc
    after_sc[...] = after

    @pl.when(jg == pl.num_programs(1) - 1)
    def _():
        s_new = jnp.sum(qrows * kn_ref[...], axis=-1, keepdims=True) - lfn_ref[...]
        m_fin = jnp.maximum(m, s_new)
        alpha = jnp.exp(m - m_fin)
        p_new = jnp.exp(s_new - m_fin)
        l_fin = alpha * l + p_new
        acc_fin = alpha * acc + p_new * vn_ref[...]
        out = jnp.where(hmask, acc_fin / l_fin, 0.0)
        o_ref[...] = jnp.sum(out, axis=0, keepdims=True)


def _fox_sample(fq, fk, fv, lf, ck_t, cv_t, clf_t, page_table, layer, w_sfx, g):
    bsz = fq.shape[0]
    n_pages = page_table.shape[1]
    n_steps = n_pages // g
    q3 = fq.reshape(bsz, 1, FOX_W)
    k3 = fk.reshape(bsz, 1, FOX_W)
    v3 = fv.reshape(bsz, 1, FOX_W)
    lf3 = lf.reshape(bsz, H_FOX, 1)
    row = pl.BlockSpec((None, 1, FOX_W), lambda b, j, pt: (b, 0, 0))

    def page_spec(rows, t):
        def imap(b, j, pt):
            return (layer, pt[b, n_pages - 1 - (j * g + t)], 0, 0)
        return pl.BlockSpec((None, None, rows, PAGE_SIZE), imap)

    in_specs = [row, row, row,
                pl.BlockSpec((None, H_FOX, 1), lambda b, j, pt: (b, 0, 0)),
                pl.BlockSpec(w_sfx.shape, lambda b, j, pt: (0, 0))]
    in_specs += [page_spec(FOX_W, t) for t in range(g)]
    in_specs += [page_spec(FOX_W, t) for t in range(g)]
    in_specs += [page_spec(H_FOX, t) for t in range(g)]
    out = pl.pallas_call(
        functools.partial(_fox_sample_kernel, pages_per_step=g),
        grid_spec=pltpu.PrefetchScalarGridSpec(
            num_scalar_prefetch=1,
            grid=(bsz, n_steps),
            in_specs=in_specs,
            out_specs=row,
            scratch_shapes=[pltpu.VMEM((H_FOX, 1), F32), pltpu.VMEM((H_FOX, 1), F32),
                            pltpu.VMEM((H_FOX, FOX_W), F32), pltpu.VMEM((H_FOX, PAGE_SIZE), F32)]),
        out_shape=jax.ShapeDtypeStruct((bsz, 1, FOX_W), F32),
        compiler_params=pltpu.CompilerParams(
            dimension_semantics=("arbitrary", "arbitrary"), vmem_limit_bytes=VMEM_LIMIT),
        name="fox_sample",
    )(page_table, q3, k3, v3, lf3, w_sfx, *([ck_t] * g), *([cv_t] * g), *([clf_t] * g))
    return out.reshape(bsz, FOX_W)


def _merge_out(x_ref, ro_ref, rg_ref, fo_ref, y_c, seg_ref, retg_ref, wout_ref):
    y_ret = jax.nn.silu(rg_ref[...]) * _head_rms(ro_ref[...], seg_ref[...], retg_ref[...])
    y = jnp.concatenate([y_ret, fo_ref[...], y_c], axis=-1).astype(BF16)
    return x_ref[...] + _dot(y, wout_ref[...])


def _merge_prompt(x_ref, ro_ref, rg_ref, fo_ref, cu_ref, cv_ref, seg_ref, retg_ref,
                  sw_ref, sb_ref, wout_ref):
    tm = x_ref.shape[0]
    r = lax.broadcasted_iota(jnp.int32, (CHUNK, CHUNK), 0)
    c = lax.broadcasted_iota(jnp.int32, (CHUNK, CHUNK), 1)
    lane = lax.broadcasted_iota(jnp.int32, (CHUNK, CMLP_W), 1)
    ws = [jnp.where(r >= c, sw_ref[g], 0.0).astype(BF16) for g in range(C_GROUPS)]
    sb = sb_ref[...]
    chunks = []
    for n in range(tm // CHUNK):
        vb = cv_ref[n * CHUNK:(n + 1) * CHUNK, :].astype(BF16)
        mixed = sb
        for g in range(C_GROUPS):
            mixed = mixed + jnp.where((lane // HEAD_DIM) == g, _dot(ws[g], vb), 0.0)
        chunks.append(cu_ref[n * CHUNK:(n + 1) * CHUNK, :] * mixed)
    y_c = chunks[0] if len(chunks) == 1 else jnp.concatenate(chunks, axis=0)
    return _merge_out(x_ref, ro_ref, rg_ref, fo_ref, y_c, seg_ref, retg_ref, wout_ref)


def _merge_sample_kernel(x_ref, ro_ref, rg_ref, fo_ref, cu_ref, cv_ref, seg_ref, retg_ref,
                         sw0_ref, sb0_ref, wout_ref, out_ref):
    y_c = cu_ref[...] * (sw0_ref[...] * cv_ref[...] + sb0_ref[...])
    out_ref[...] = _merge_out(x_ref, ro_ref, rg_ref, fo_ref, y_c, seg_ref, retg_ref, wout_ref)


def _merge_specs(tm, d, wout, sample, row, const):
    if sample:
        sw_spec = pl.BlockSpec((1, CMLP_W), const)
        sb_spec = pl.BlockSpec((1, CMLP_W), const)
    else:
        sw_spec = pl.BlockSpec((C_GROUPS, CHUNK, CHUNK), lambda i: (0, 0, 0))
        sb_spec = pl.BlockSpec((CHUNK, CMLP_W), const)
    return [pl.BlockSpec((tm, d), row),
            pl.BlockSpec((tm, RET_W), row), pl.BlockSpec((tm, RET_W), row),
            pl.BlockSpec((tm, FOX_W), row),
            pl.BlockSpec((tm, CMLP_W), row), pl.BlockSpec((tm, CMLP_W), row),
            pl.BlockSpec((256, 256), const), pl.BlockSpec((1, RET_W), const),
            sw_spec, sb_spec,
            _layer_spec(wout, wout[0].shape[1:], const)]


def _merge_sample(x2d, ro, rg, fo, cu, cv, seg, retg, sw0, sb0, wout):
    n, d = x2d.shape
    row = lambda i: (i, 0)
    const = lambda i: (0, 0)
    return pl.pallas_call(
        _merge_sample_kernel,
        grid=(1,),
        in_specs=_merge_specs(n, d, wout, True, row, const),
        out_specs=pl.BlockSpec((n, d), row),
        out_shape=jax.ShapeDtypeStruct((n, d), F32),
        compiler_params=pltpu.CompilerParams(
            dimension_semantics=("arbitrary",), vmem_limit_bytes=VMEM_LIMIT),
        name="merge_sample",
    )(x2d, ro, rg, fo, cu, cv, seg, retg, sw0, sb0, wout[0])


def _ffn_prompt_kernel(*refs, tiles_per_seq, fb):
    merge_refs = refs[:11]
    g2_ref, wu_ref, cw_ref, cb_ref, wd_ref, prev_ref, out_ref, st_ref, act_sc, carry_sc = refs[11:]
    t = pl.program_id(0) % tiles_per_seq
    tm = out_ref.shape[0]
    d_ff = wd_ref.shape[0]

    @pl.when(t == 0)
    def _():
        carry_sc[...] = prev_ref[...]

    x = _merge_prompt(*merge_refs)
    hb = _row_rms(x, g2_ref[...]).astype(BF16)
    row = lax.broadcasted_iota(jnp.int32, (tm, fb), 0)

    def conv_cols(lo):
        u = _dot(hb, wu_ref[:, lo:lo + fb])
        prev = carry_sc[:, lo:lo + fb]
        s1 = jnp.where(row == 0, prev[1:2], pltpu.roll(u, 1, 0))
        s2 = jnp.where(row == 0, prev[0:1], jnp.where(row == 1, prev[1:2], pltpu.roll(u, 2, 0)))
        carry_sc[:, lo:lo + fb] = u[tm - (CONV_W - 1):tm, :]
        return (((cb_ref[:, lo:lo + fb] + cw_ref[0:1, lo:lo + fb] * s2)
                 + cw_ref[1:2, lo:lo + fb] * s1) + cw_ref[2:3, lo:lo + fb] * u)

    for c in range(d_ff // fb):
        gate = conv_cols(c * fb)
        val = conv_cols(d_ff + c * fb)
        act_sc[:, c * fb:(c + 1) * fb] = (jax.nn.silu(gate) * val).astype(BF16)
    st_ref[...] = carry_sc[...]
    out_ref[...] = x + _dot(act_sc[...], wd_ref[...])


def _ffn_sample_kernel(x_ref, g2_ref, wug_ref, wuv_ref, cwg_ref, cwv_ref, cbg_ref, cbv_ref, wd_ref,
                       p0g_ref, p0v_ref, p1g_ref, p1v_ref, out_ref, st_ref, h_sc, acc_sc, *, d_ff):
    j = pl.program_id(1)
    fb = wug_ref.shape[1]

    @pl.when(j == 0)
    def _():
        h_sc[...] = _row_rms(x_ref[...], g2_ref[...]).astype(BF16)
        acc_sc[...] = jnp.zeros_like(acc_sc)

    hb = h_sc[...]

    def conv(u, p0, p1, w_ref, b_ref):
        return ((b_ref[...] + w_ref[0:1] * p0) + w_ref[1:2] * p1) + w_ref[2:3] * u

    ug = _dot(hb, wug_ref[...])
    uv = _dot(hb, wuv_ref[...])
    p1g = p1g_ref[...]
    p1v = p1v_ref[...]
    conv_g = conv(ug, p0g_ref[...], p1g, cwg_ref, cbg_ref)
    conv_v = conv(uv, p0v_ref[...], p1v, cwv_ref, cbv_ref)
    cg = pl.multiple_of(j * fb, LANES)
    cv = pl.multiple_of(d_ff + j * fb, LANES)
    st_ref[:, pl.ds(cg, fb)] = p1g
    st_ref[:, pl.ds(cv, fb)] = p1v
    st_ref[:, pl.ds(pl.multiple_of(2 * d_ff + j * fb, LANES), fb)] = ug
    st_ref[:, pl.ds(pl.multiple_of(3 * d_ff + j * fb, LANES), fb)] = uv
    act = (jax.nn.silu(conv_g) * conv_v).astype(BF16)
    acc_sc[...] += _dot(act, wd_ref[...])

    @pl.when(j == pl.num_programs(1) - 1)
    def _():
        out_ref[...] = x_ref[...] + acc_sc[...]


def _ffn(x2d, g2, wup, cw, cb, wdown, prev, tm, fb, seq, merge_args=None):
    n, d = x2d.shape
    d_ff = wdown[0].shape[1]
    nff = d_ff // fb
    nt = n // tm
    xspec = pl.BlockSpec((tm, d), lambda i, j: (i, 0))
    common = [xspec,
              pl.BlockSpec((1, d), lambda i, j: (0, 0)),
              _layer_spec(wup, (d, fb), lambda i, j: (0, j)),
              _layer_spec(wup, (d, fb), lambda i, j: (0, nff + j)),
              pl.BlockSpec((CONV_W, fb), lambda i, j: (0, j)),
              pl.BlockSpec((CONV_W, fb), lambda i, j: (0, nff + j)),
              pl.BlockSpec((1, fb), lambda i, j: (0, j)),
              pl.BlockSpec((1, fb), lambda i, j: (0, nff + j)),
              _layer_spec(wdown, (fb, d), lambda i, j: (j, 0))]
    scratch = [pltpu.VMEM((tm, d), BF16), pltpu.VMEM((tm, d), F32)]
    params = pltpu.CompilerParams(
        dimension_semantics=("arbitrary", "arbitrary"), vmem_limit_bytes=VMEM_LIMIT)
    if seq == 1:
        bsz = n
        prev2 = prev.reshape(bsz, (CONV_W - 1) * 2 * d_ff)
        pspecs = [pl.BlockSpec((bsz, fb), lambda i, j, o=o: (0, o + j))
                  for o in (0, nff, 2 * nff, 3 * nff)]
        out, st = pl.pallas_call(
            functools.partial(_ffn_sample_kernel, d_ff=d_ff),
            grid=(nt, nff),
            in_specs=common + [pspecs[0], pspecs[1], pspecs[2], pspecs[3]],
            out_specs=[xspec, pl.BlockSpec(prev2.shape, lambda i, j: (0, 0))],
            out_shape=[jax.ShapeDtypeStruct((n, d), F32), jax.ShapeDtypeStruct(prev2.shape, F32)],
            scratch_shapes=scratch,
            compiler_params=params,
            name="ffn_sample",
        )(x2d, g2, wup[0], wup[0], cw, cw, cb, cb, wdown[0], prev2, prev2, prev2, prev2)
        return out, st.reshape(prev.shape)
    tiles_per_seq = seq // tm
    const = lambda i: (0, 0)
    resident = lambda w: _layer_spec(w, w[0].shape[1:], const, pipeline_mode=pl.Buffered(1))
    state = pl.BlockSpec((None, CONV_W - 1, 2 * d_ff), lambda i: (i // tiles_per_seq, 0, 0))
    row = pl.BlockSpec((tm, d), lambda i: (i, 0))
    wout = merge_args[-1]
    merge_specs = _merge_specs(tm, d, wout, False, lambda i: (i, 0), const)
    merge_specs[-1] = resident(wout)
    return pl.pallas_call(
        functools.partial(_ffn_prompt_kernel, tiles_per_seq=tiles_per_seq, fb=fb),
        grid=(nt,),
        in_specs=merge_specs + [
            pl.BlockSpec((1, d), const),
            resident(wup),
            pl.BlockSpec(cw.shape, const),
            pl.BlockSpec(cb.shape, const),
            resident(wdown),
            state],
        out_specs=[row, state],
        out_shape=[jax.ShapeDtypeStruct((n, d), F32), jax.ShapeDtypeStruct(prev.shape, F32)],
        scratch_shapes=[pltpu.VMEM((tm, d_ff), BF16), pltpu.VMEM((CONV_W - 1, 2 * d_ff), F32)],
        compiler_params=pltpu.CompilerParams(
            dimension_semantics=("arbitrary",), vmem_limit_bytes=VMEM_LIMIT),
        name="ffn_prompt",
    )(x2d, *merge_args[:-1], wout[0], g2, wup[0], cw, cb, wdown[0], prev)


def _rope_tables(pos):
    half = HEAD_DIM // 2
    inv = 1.0 / (ROPE_BASE ** (jnp.arange(half, dtype=F32) / half))
    ang = pos.astype(F32)[:, None] * inv[None, :]
    cos = jnp.cos(ang)
    sin = jnp.sin(ang)
    cos_e = jnp.tile(jnp.concatenate([cos, cos], axis=-1), (1, H_RET))
    sin_e = jnp.tile(jnp.concatenate([-sin, sin], axis=-1), (1, H_RET))
    return cos_e, sin_e


def _retention_tables():
    c = CHUNK
    lg = jnp.log1p(-jnp.exp2(-5.0 - jnp.arange(H_RET, dtype=F32)))
    idx = jnp.arange(c, dtype=F32)
    diff = idx[:, None] - idx[None, :]
    intra = jnp.where(diff[None] >= 0, jnp.exp(jnp.maximum(diff, 0.0)[None] * lg[:, None, None]), 0.0)
    q_dec = jnp.exp((idx[None, :] + 1.0) * lg[:, None])
    k_dec = jnp.exp((c - 1.0 - idx[None, :]) * lg[:, None])
    chunk_dec = jnp.exp(c * lg)
    expand = lambda a: jnp.repeat(a.T, HEAD_DIM, axis=1)
    head_of = jnp.arange(RET_W) // HEAD_DIM
    bd = (head_of[:, None] == head_of[None, :]).astype(F32)
    decmat = jnp.broadcast_to(chunk_dec[head_of][:, None], (RET_W, RET_W))
    gamma = jnp.exp(1.0 * lg)
    gcol = jnp.broadcast_to(gamma[:, None, None], (H_RET, HEAD_DIM, 1))
    grow = jnp.broadcast_to(gamma[:, None, None], (H_RET, 1, HEAD_DIM))
    return dict(intra=intra, qdec=expand(q_dec), kdec=expand(k_dec), decmat=decmat, bd=bd,
                gcol=gcol, grow=grow)


def _pick_tile(n, target):
    t = min(n, target)
    while n % t:
        t //= 2
    return t


@jax.jit
def kernel(x_prompt, x_sample, cache_k, cache_v, cache_logf, state_ret, state_ffn_conv, page_table,
           norm1_g, w_in, b_forget, ret_norm_g, q_norm_g, k_norm_g, spatial_w, spatial_b, w_out,
           norm2_g, w_up, conv_w, conv_b, w_down):
    bp, lp, d = x_prompt.shape
    bs, ls, _ = x_sample.shape
    depth = w_in.shape[0]
    d_ff = w_down.shape[1]
    n_pages = page_table.shape[1]
    past_len = n_pages * PAGE_SIZE
    assert ls == 1 and lp % CHUNK == 0

    fz0 = MAIN_COLS
    w_t = jnp.swapaxes(w_in, 1, 2)
    w_in_b = w_t.astype(BF16)
    w_tail = jnp.concatenate(
        [w_t[:, fz0 + H_FOX:], w_t[:, fz0:fz0 + H_FOX],
         jnp.zeros((depth, FZ_COLS - H_FOX, d), w_in.dtype)], axis=1).astype(BF16)
    w_out_b = w_out.astype(BF16)
    w_up_b = w_up.astype(BF16)
    w_down_b = w_down.astype(BF16)
    bf_p = jnp.pad(b_forget, ((0, 0), (0, FZ_COLS - H_FOX)))
    qg_e = jnp.tile(q_norm_g, (1, H_FOX))
    kg_e = jnp.tile(k_norm_g, (1, H_FOX))
    retg_e = ret_norm_g.reshape(depth, RET_W)
    sb_e = jnp.repeat(jnp.swapaxes(spatial_b, 1, 2), HEAD_DIM, axis=2)
    sw0_e = jnp.repeat(spatial_w[:, :, 0, 0], HEAD_DIM, axis=1)
    sb0_e = jnp.repeat(spatial_b[:, :, 0], HEAD_DIM, axis=1)

    head_of = jnp.arange(256) // HEAD_DIM
    seg = (head_of[:, None] == head_of[None, :]).astype(BF16)
    idx = jnp.arange(CHUNK)
    tri_incl = (idx[:, None] >= idx[None, :]).astype(BF16)
    sfx = (idx[:, None] > idx[None, :]).astype(BF16)
    w_sfx = jnp.concatenate([sfx, jnp.ones((PAGE_SIZE, PAGE_SIZE), BF16)], axis=1)
    ck_t = jnp.transpose(cache_k, (0, 1, 3, 4, 2)).reshape(depth, -1, FOX_W, PAGE_SIZE)
    cv_t = jnp.transpose(cache_v, (0, 1, 3, 4, 2)).reshape(depth, -1, FOX_W, PAGE_SIZE)
    clf_t = jnp.transpose(cache_logf, (0, 1, 3, 2))
    cos_p, sin_p = _rope_tables(jnp.arange(lp))
    cos_s, sin_s = _rope_tables(jnp.broadcast_to(past_len + jnp.arange(ls), (bs,)))
    tabs = _retention_tables()

    tm_p = _pick_tile(lp, 512)
    tq = _pick_tile(lp, 256)
    tm_f = _pick_tile(lp, 512)
    fb = 256
    fb_s = d_ff // 2 if (d_ff // 2) % LANES == 0 else fb
    g_pages = _pick_tile(n_pages, 32)
    zeros_prev = jnp.zeros((bp, CONV_W - 1, 2 * d_ff), F32)

    xp = x_prompt.reshape(bp * lp, d)
    xs = x_sample.reshape(bs * ls, d)
    outs = [[] for _ in range(11)]
    fkt = fvt = None
    for l in range(depth):
        g1 = norm1_g[l][None]
        g2 = norm2_g[l][None]
        bfl = bf_p[l][None]
        qg = qg_e[l][None]
        kg = kg_e[l][None]
        retg = retg_e[l][None]
        cb = conv_b[l][None]

        rq, rk, rv, rg, fq, fkt, fvt, _, lfp, cu, cv = _proj(
            xp, cos_p, sin_p, g1, (w_in_b, l), (w_tail, l), bfl, qg, kg, seg, tm_p, (depth, l, fkt, fvt))
        ro, r_state = _ret_prompt(rq, rk, rv, tabs, bp, lp)
        cum, cumt, lft = _cum(lfp, tri_incl, bp, lp)
        fo = _fox_prompt(fq, fkt, fvt, l, cum, cumt, bp, lp, tq, 2)
        xp, c_state = _ffn(xp, g2, (w_up_b, l), conv_w[l], cb, (w_down_b, l), zeros_prev, tm_f, fb, lp,
                           (ro, rg, fo, cu, cv, seg, retg, spatial_w[l], sb_e[l], (w_out_b, l)))
        outs[2].append(lft)
        outs[3].append(r_state)
        outs[4].append(c_state)

        rq, rk, rv, rg, fq, fk, fv, lf, lfp, cu, cv = _proj(
            xs, cos_s, sin_s, g1, (w_in_b, l), (w_tail, l), bfl, qg, kg, seg, bs)
        ro, r_state = _ret_sample(rq, rk, rv, state_ret, l, tabs)
        fo = _fox_sample(fq, fk, fv, lf, ck_t, cv_t, clf_t, page_table, l, w_sfx, g_pages)
        xs = _merge_sample(xs, ro, rg, fo, cu, cv, seg, retg, sw0_e[l][None], sb0_e[l][None],
                           (w_out_b, l))
        xs, c_state = _ffn(xs, g2, (w_up_b, l), conv_w[l], cb, (w_down_b, l), state_ffn_conv[l], bs, fb_s, 1)
        outs[5].append(fk.reshape(bs, ls, H_FOX, HEAD_DIM))
        outs[6].append(fv.reshape(bs, ls, H_FOX, HEAD_DIM))
        outs[7].append(lf.reshape(bs, ls, H_FOX))
        outs[8].append(r_state)
        outs[9].append(cv.reshape(bs, ls, CMLP_W))
        outs[10].append(c_state)

    stacked = [jnp.stack(o) if o else None for o in outs]
    for o, buf in ((0, fkt), (1, fvt)):
        stacked[o] = jnp.transpose(buf.reshape(depth, bp, H_FOX, HEAD_DIM, lp), (0, 1, 4, 2, 3))
    stacked[2] = jnp.transpose(stacked[2], (0, 1, 3, 2))
    return (xp.reshape(bp, lp, d), xs.reshape(bs, ls, d), *stacked)
```
